```python
import jax, jax.numpy as jnp
from jax import lax
import numpy as np

D_MODEL = 1024
BATCH = 8
SEQ = 2048
DEPTH = 2
DEC_BATCH = 128
DEC_SEQ = 8
PAST_LEN = 16384
PAGE_SIZE = 128

N_MEM = 256
XA_HEADS = 4
XA_HEAD_DIM = D_MODEL // XA_HEADS
D_A = D_MODEL
CONV_A_WIDTH = 31
D_B = D_MODEL
LRU_BLOCKS = 8
LRU_BLOCK = D_B // LRU_BLOCKS
CONV_B_WIDTH = 4
LRU_C = 8.0
D_FF = 2816
OFF_BX = 2 * D_A
OFF_BG = OFF_BX + D_B
OFF_GA = OFF_BG + D_B
OFF_GB = OFF_GA + D_MODEL
D_IN = OFF_GB + D_MODEL
N_NORMS = 9
EPS = 1e-6

kernel_name = 'hybrid_conv_rglru_gated_decoder_step'


def rms_norm(x, g):
    xf = x.astype(jnp.float32)
    y = xf * lax.rsqrt(jnp.mean(xf * xf, axis=-1, keepdims=True) + EPS)
    return (y * g.astype(jnp.float32)).astype(x.dtype)


def layer_norm(x, g, b):
    xf = x.astype(jnp.float32)
    mu = jnp.mean(xf, axis=-1, keepdims=True)
    var = jnp.mean(jnp.square(xf - mu), axis=-1, keepdims=True)
    y = (xf - mu) * lax.rsqrt(var + EPS)
    return (y * g.astype(jnp.float32) + b.astype(jnp.float32)).astype(x.dtype)


def swiglu(x, w_in, w_out):
    gate, up = jnp.split(x @ w_in, 2, axis=-1)
    return (jax.nn.silu(gate) * up) @ w_out


def causal_dw_conv(u, buf, w, b):
    full = jnp.concatenate([buf.astype(u.dtype), u], axis=1)
    y = lax.conv_general_dilated(full, w[:, None, :].astype(u.dtype), window_strides=(1,), padding='VALID',
                                 dimension_numbers=('NWC', 'WIO', 'NWC'), feature_group_count=u.shape[-1])
    return y + b, full[:, -buf.shape[1]:]


def _lru_combine(c1, c2):
    a1, b1 = c1
    a2, b2 = c2
    return a1 * a2, a2 * b1 + b2


def rg_lru(xb, h0, w_a, b_a, w_x, b_x, lam):
    B, T, _ = xb.shape
    xf = xb.astype(jnp.float32)
    xh = xf.reshape(B, T, LRU_BLOCKS, LRU_BLOCK)
    r = jax.nn.sigmoid(jnp.einsum('btni,nij->btnj', xh, w_a.astype(jnp.float32)).reshape(B, T, D_B) + b_a)
    i = jax.nn.sigmoid(jnp.einsum('btni,nij->btnj', xh, w_x.astype(jnp.float32)).reshape(B, T, D_B) + b_x)
    log_a = -LRU_C * r * jax.nn.softplus(-lam.astype(jnp.float32))
    a = jnp.exp(log_a)
    u = jnp.sqrt(-jnp.expm1(2.0 * log_a)) * (i * xf)
    a_cum, b_cum = lax.associative_scan(_lru_combine, (a, u), axis=1)
    h = a_cum * h0.astype(jnp.float32)[:, None, :] + b_cum
    return h.astype(xb.dtype), h[:, -1].astype(h0.dtype)


def mixer(u, buf_a, buf_b, h0, p, l):
    proj = u @ p['w_in'][l] + p['b_in'][l]
    a_in, b_x, b_g, g_a, g_b = jnp.split(proj, [OFF_BX, OFF_BG, OFF_GA, OFF_GB], axis=-1)
    a_glu = a_in[..., :D_A] * jax.nn.sigmoid(a_in[..., D_A:])
    a_conv, new_buf_a = causal_dw_conv(a_glu, buf_a, p['conv_a_w'][l], p['conv_a_b'][l])
    y_a = jax.nn.silu(layer_norm(a_conv, p['conv_ln_g'][l], p['conv_ln_b'][l])) @ p['w_a_out'][l]
    b_conv, new_buf_b = causal_dw_conv(b_x, buf_b, p['conv_b_w'][l], p['conv_b_b'][l])
    h, h_last = rg_lru(b_conv, h0, p['lru_w_a'][l], p['lru_b_a'][l], p['lru_w_x'][l], p['lru_b_x'][l], p['lru_lambda'][l])
    y_b = (h * jax.nn.gelu(b_g)) @ p['w_b_out'][l]
    merged = jax.nn.sigmoid(g_a) * y_a + jax.nn.sigmoid(g_b) * y_b
    return merged @ p['w_out'][l], new_buf_a, new_buf_b, h_last


def mem_kv(mem, g, w_k, w_v):
    m = rms_norm(mem, g)
    return jnp.einsum('bmd,dhk->bmhk', m, w_k), jnp.einsum('bmd,dhk->bmhk', m, w_v)


def cross_attn(u, k, v, w_q, w_o):
    q = jnp.einsum('btd,dhk->bthk', u, w_q)
    s = jnp.einsum('bthk,bmhk->bhtm', q.astype(jnp.float32), k.astype(jnp.float32)) * (XA_HEAD_DIM ** -0.5)
    pr = jax.nn.softmax(s, axis=-1).astype(v.dtype)
    o = jnp.einsum('bhtm,bmhk->bthk', pr, v)
    return jnp.einsum('bthk,hkd->btd', o, w_o)


def trunk(x, mem_k, mem_v, buf_a, buf_b, h0, p):
    new_a, new_b, new_h = [], [], []
    for l in range(DEPTH):
        g = p['norm_g'][l]
        x = x + 0.5 * rms_norm(swiglu(rms_norm(x, g[0]), p['ffn1_w_in'][l], p['ffn1_w_out'][l]), g[1])
        m, nba, nbb, hl = mixer(rms_norm(x, g[2]), buf_a[l], buf_b[l], h0[l], p, l)
        x = x + rms_norm(m, g[3])
        x = x + rms_norm(cross_attn(rms_norm(x, g[4]), mem_k[l], mem_v[l], p['xa_w_q'][l], p['xa_w_o'][l]), g[5])
        x = x + 0.5 * rms_norm(swiglu(rms_norm(x, g[7]), p['ffn2_w_in'][l], p['ffn2_w_out'][l]), g[8])
        new_a.append(nba)
        new_b.append(nbb)
        new_h.append(hl)
    return x, jnp.stack(new_a), jnp.stack(new_b), jnp.stack(new_h)


def setup_inputs(seed: int = 0) -> dict:
    key = jax.random.key(seed)
    ks = jax.random.split(key, 34)
    f32 = jnp.float32
    D = D_MODEL

    def nrm(k, shape, scale):
        return scale * jax.random.normal(k, shape, f32)

    a0 = jax.random.uniform(ks[21], (DEPTH, D_B), f32, 0.9, 0.999)
    s0 = a0 ** (1.0 / LRU_C)
    return {
        'x_prompt': nrm(ks[0], (BATCH, SEQ, D), 1.0),
        'x_sample': nrm(ks[1], (DEC_BATCH, DEC_SEQ, D), 1.0),
        'mem_prompt': nrm(ks[2], (BATCH, N_MEM, D), 1.0),
        'cache_mem_k': nrm(ks[3], (DEPTH, DEC_BATCH, N_MEM, XA_HEADS, XA_HEAD_DIM), 1.0),
        'cache_mem_v': nrm(ks[4], (DEPTH, DEC_BATCH, N_MEM, XA_HEADS, XA_HEAD_DIM), 1.0),
        'state_conv_a': nrm(ks[5], (DEPTH, DEC_BATCH, CONV_A_WIDTH - 1, D_A), 0.5),
        'state_conv_b': nrm(ks[6], (DEPTH, DEC_BATCH, CONV_B_WIDTH - 1, D_B), 1.0),
        'state_lru_h': nrm(ks[7], (DEPTH, DEC_BATCH, D_B), 0.5),
        'w_in': nrm(ks[8], (DEPTH, D, D_IN), D ** -0.5),
        'b_in': nrm(ks[9], (DEPTH, D_IN), 0.01),
        'conv_a_w': nrm(ks[10], (DEPTH, CONV_A_WIDTH, D_A), CONV_A_WIDTH ** -0.5),
        'conv_a_b': nrm(ks[11], (DEPTH, D_A), 0.01),
        'conv_ln_g': 1.0 + nrm(ks[12], (DEPTH, D_A), 0.02),
        'conv_ln_b': nrm(ks[13], (DEPTH, D_A), 0.01),
        'w_a_out': nrm(ks[14], (DEPTH, D_A, D), D_A ** -0.5),
        'conv_b_w': nrm(ks[15], (DEPTH, CONV_B_WIDTH, D_B), CONV_B_WIDTH ** -0.5),
        'conv_b_b': nrm(ks[16], (DEPTH, D_B), 0.01),
        'lru_w_a': nrm(ks[17], (DEPTH, LRU_BLOCKS, LRU_BLOCK, LRU_BLOCK), LRU_BLOCK ** -0.5),
        'lru_b_a': nrm(ks[18], (DEPTH, D_B), 0.01),
        'lru_w_x': nrm(ks[19], (DEPTH, LRU_BLOCKS, LRU_BLOCK, LRU_BLOCK), LRU_BLOCK ** -0.5),
        'lru_b_x': nrm(ks[20], (DEPTH, D_B), 0.01),
        'lru_lambda': jnp.log(s0) - jnp.log1p(-s0),
        'w_b_out': nrm(ks[22], (DEPTH, D_B, D), D_B ** -0.5),
        'w_out': nrm(ks[23], (DEPTH, D, D), D ** -0.5),
        'xa_w_q': nrm(ks[24], (DEPTH, D, XA_HEADS, XA_HEAD_DIM), D ** -0.5),
        'xa_w_k': nrm(ks[25], (DEPTH, D, XA_HEADS, XA_HEAD_DIM), D ** -0.5),
        'xa_w_v': nrm(ks[26], (DEPTH, D, XA_HEADS, XA_HEAD_DIM), D ** -0.5),
        'xa_w_o': nrm(ks[27], (DEPTH, XA_HEADS, XA_HEAD_DIM, D), D ** -0.5),
        'ffn1_w_in': nrm(ks[28], (DEPTH, D, 2 * D_FF), D ** -0.5),
        'ffn1_w_out': nrm(ks[29], (DEPTH, D_FF, D), D_FF ** -0.5),
        'ffn2_w_in': nrm(ks[30], (DEPTH, D, 2 * D_FF), D ** -0.5),
        'ffn2_w_out': nrm(ks[31], (DEPTH, D_FF, D), D_FF ** -0.5),
        'norm_g': 1.0 + nrm(ks[32], (DEPTH, N_NORMS, D), 0.02),
    }


def reference(x_prompt, x_sample, mem_prompt, cache_mem_k, cache_mem_v, state_conv_a, state_conv_b, state_lru_h,
              w_in, b_in, conv_a_w, conv_a_b, conv_ln_g, conv_ln_b, w_a_out, conv_b_w, conv_b_b,
              lru_w_a, lru_b_a, lru_w_x, lru_b_x, lru_lambda, w_b_out, w_out,
              xa_w_q, xa_w_k, xa_w_v, xa_w_o, ffn1_w_in, ffn1_w_out, ffn2_w_in, ffn2_w_out, norm_g):
    p = dict(w_in=w_in, b_in=b_in, conv_a_w=conv_a_w, conv_a_b=conv_a_b, conv_ln_g=conv_ln_g, conv_ln_b=conv_ln_b,
             w_a_out=w_a_out, conv_b_w=conv_b_w, conv_b_b=conv_b_b, lru_w_a=lru_w_a, lru_b_a=lru_b_a,
             lru_w_x=lru_w_x, lru_b_x=lru_b_x, lru_lambda=lru_lambda, w_b_out=w_b_out, w_out=w_out,
             xa_w_q=xa_w_q, xa_w_o=xa_w_o, ffn1_w_in=ffn1_w_in, ffn1_w_out=ffn1_w_out,
             ffn2_w_in=ffn2_w_in, ffn2_w_out=ffn2_w_out, norm_g=norm_g)
    mk, mv = [], []
    for l in range(DEPTH):
        k, v = mem_kv(mem_prompt, norm_g[l, 6], xa_w_k[l], xa_w_v[l])
        mk.append(k)
        mv.append(v)
    mem_k_prompt = jnp.stack(mk)
    mem_v_prompt = jnp.stack(mv)
    bp = x_prompt.shape[0]
    zero_a = jnp.zeros((DEPTH, bp, CONV_A_WIDTH - 1, D_A), x_prompt.dtype)
    zero_b = jnp.zeros((DEPTH, bp, CONV_B_WIDTH - 1, D_B), x_prompt.dtype)
    zero_h = jnp.zeros((DEPTH, bp, D_B), state_lru_h.dtype)
    y_prompt, conv_a_prompt, conv_b_prompt, lru_h_prompt = trunk(
        x_prompt, mem_k_prompt, mem_v_prompt, zero_a, zero_b, zero_h, p)
    y_sample, conv_a_sample, conv_b_sample, lru_h_sample = trunk(
        x_sample, cache_mem_k, cache_mem_v, state_conv_a, state_conv_b, state_lru_h, p)
    return (y_prompt, y_sample, mem_k_prompt, mem_v_prompt, conv_a_prompt, conv_b_prompt, lru_h_prompt,
            conv_a_sample, conv_b_sample, lru_h_sample)
```

```python
import functools
import math

import jax
import jax.numpy as jnp
from jax import lax
from jax.experimental import pallas as pl
from jax.experimental.pallas import tpu as pltpu

D = 1024
DEPTH = 2
N_MEM = 256
HEADS = 4
HEAD_DIM = D // HEADS
CONV_A = 31
CONV_B = 4
LRU_BLOCKS = 8
LRU_BLOCK = D // LRU_BLOCKS
LRU_C = 8.0
D_FF = 2816
EPS = 1e-6

LANES = 128
SUBLANES = 8
MXU_N = 256
FF_CHUNK = MXU_N
N_FF_CHUNKS = D_FF // FF_CHUNK
VMEM_LIMIT = 56 * 1024 * 1024

F32 = jnp.float32
BF16 = jnp.bfloat16


def _dot(a, b):
    return jnp.dot(a, b, preferred_element_type=F32)


def _rms(x, g):
    ms = jnp.mean(x * x, axis=-1, keepdims=True)
    return x * lax.rsqrt(ms + EPS) * g


def _sigmoid(x):
    return 1.0 / (1.0 + jnp.exp(-x))


def _resident(shape):
    nd = len(shape)
    return pl.BlockSpec(shape, lambda *_: (0,) * nd, pipeline_mode=pl.Buffered(1))


def _params(sem):
    return pltpu.CompilerParams(dimension_semantics=sem, vmem_limit_bytes=VMEM_LIMIT)


def _ffn_body(x_ref, g_ref, wg_ref, wu_ref, wo_ref, o_ref):
    x = x_ref[...]
    xn = _rms(x, g_ref[0:1, :]).astype(BF16)
    acc = None
    for j in range(N_FF_CHUNKS):
        gate = _dot(xn, wg_ref[j])
        up = _dot(xn, wu_ref[j])
        h = (gate * _sigmoid(gate) * up).astype(BF16)
        d = _dot(h, wo_ref[j])
        acc = d if acc is None else acc + d
    o_ref[...] = x + 0.5 * _rms(acc, g_ref[1:2, :])


def _ffn(x_arr, x_spec, out_shape, out_spec, grid, g, wg, wu, wo):
    return pl.pallas_call(
        _ffn_body,
        grid=grid,
        in_specs=[x_spec, _resident(g.shape), _resident(wg.shape), _resident(wu.shape),
                  _resident(wo.shape)],
        out_specs=out_spec,
        out_shape=jax.ShapeDtypeStruct(out_shape, F32),
        compiler_params=_params(("arbitrary",) * len(grid)),
        name="ffn",
    )(x_arr, g, wg, wu, wo)


def _memkv_body(m_ref, g_ref, wk_ref, wv_ref, k_ref, v_ref):
    mn = _rms(m_ref[...], g_ref[...]).astype(BF16)
    k_ref[...] = _dot(mn, wk_ref[...])
    v_ref[...] = _dot(mn, wv_ref[...])


def _memkv(mem2d, g6, wk, wv, tm=512):
    rows = mem2d.shape[0]
    w_spec = pl.BlockSpec((None, D, D), lambda l, i: (l, 0, 0))
    o_spec = pl.BlockSpec((None, tm, D), lambda l, i: (l, i, 0))
    return pl.pallas_call(
        _memkv_body,
        grid=(DEPTH, rows // tm),
        in_specs=[pl.BlockSpec((tm, D), lambda l, i: (i, 0)),
                  pl.BlockSpec((None, 1, D), lambda l, i: (l, 0, 0)),
                  w_spec, w_spec],
        out_specs=[o_spec, o_spec],
        out_shape=[jax.ShapeDtypeStruct((DEPTH, rows, D), F32)] * 2,
        compiler_params=_params(("arbitrary", "arbitrary")),
        name="memkv",
    )(mem2d, g6, wk, wv)


def _attend(q, k, v):
    outs = []
    for h in range(HEADS):
        hs = slice(h * HEAD_DIM, (h + 1) * HEAD_DIM)
        s = lax.dot_general(q[:, hs].astype(k.dtype), k[:, hs], (((1,), (1,)), ((), ())),
                            preferred_element_type=F32)
        e = jnp.exp(s - jnp.max(s, axis=-1, keepdims=True))
        p = e * (1.0 / jnp.sum(e, axis=-1, keepdims=True))
        outs.append(_dot(p.astype(v.dtype), v[:, hs]))
    return jnp.concatenate(outs, axis=-1)


def _xattn_prompt_body(x_ref, k_ref, v_ref, g_ref, wq_ref, wo_ref, o_ref):
    x = x_ref[...]
    u = _rms(x, g_ref[0:1, :]).astype(BF16)
    q = _dot(u, wq_ref[...]) * (HEAD_DIM ** -0.5)
    o = _attend(q, k_ref[...].astype(BF16), v_ref[...].astype(BF16))
    y = _dot(o.astype(BF16), wo_ref[...])
    o_ref[...] = x + _rms(y, g_ref[1:2, :])


def _xattn_prompt(x_tm, k, v, g, wq, wo, n_batch, tt=512):
    n_t = x_tm.shape[0]
    x_spec = pl.BlockSpec((tt, D), lambda b, i: (i, b))
    kv_spec = pl.BlockSpec((N_MEM, D), lambda b, i: (b, 0))
    return pl.pallas_call(
        _xattn_prompt_body,
        grid=(n_batch, n_t // tt),
        in_specs=[x_spec, kv_spec, kv_spec, _resident(g.shape), _resident(wq.shape),
                  _resident(wo.shape)],
        out_specs=x_spec,
        out_shape=jax.ShapeDtypeStruct(x_tm.shape, F32),
        compiler_params=_params(("arbitrary", "arbitrary")),
        name="xattn_prompt",
    )(x_tm, k, v, g, wq, wo)


def _xattn_sample_body(x_ref, k_ref, v_ref, g_ref, wq_ref, wo_ref, o_ref, *, bb, tq):
    x = x_ref[...]
    u = _rms(x, g_ref[0:1, :]).astype(BF16)
    q = _dot(u, wq_ref[...]) * (HEAD_DIM ** -0.5)
    outs = [_attend(q[b * tq:(b + 1) * tq], k_ref[b], v_ref[b]) for b in range(bb)]
    o = jnp.concatenate(outs, axis=0)
    y = _dot(o.astype(BF16), wo_ref[...])
    o_ref[...] = x + _rms(y, g_ref[1:2, :])


def _xattn_sample(x_bm, k, v, g, wq, wo, tq, bb=8):
    n_batch = k.shape[0]
    x_spec = pl.BlockSpec((bb * tq, D), lambda i: (i, 0))
    kv_spec = pl.BlockSpec((bb, N_MEM, D), lambda i: (i, 0, 0))
    return pl.pallas_call(
        functools.partial(_xattn_sample_body, bb=bb, tq=tq),
        grid=(n_batch // bb,),
        in_specs=[x_spec, kv_spec, kv_spec, _resident(g.shape), _resident(wq.shape),
                  _resident(wo.shape)],
        out_specs=x_spec,
        out_shape=jax.ShapeDtypeStruct(x_bm.shape, F32),
        compiler_params=_params(("arbitrary",)),
        name="xattn_sample",
    )(x_bm, k, v, g, wq, wo)


CONV_ROWS = 64


def _causal_conv(buf_ref, w_ref, b_ref, out_ref, n_taps, rows, nb):
    for c in range(D // LANES):
        cs = slice(c * LANES, (c + 1) * LANES)
        wk = [jnp.broadcast_to(w_ref[k:k + 1, cs], (CONV_ROWS, LANES)) for k in range(n_taps)]
        bias = jnp.broadcast_to(b_ref[0:1, cs], (CONV_ROWS, LANES))

        def body(i, carry):
            r0 = pl.multiple_of(i * CONV_ROWS, CONV_ROWS)
            acc = bias
            for k in range(n_taps):
                acc = acc + wk[k] * buf_ref[pl.ds(r0 + k * nb, CONV_ROWS), cs]
            out_ref[pl.ds(r0, CONV_ROWS), cs] = acc
            return carry

        lax.fori_loop(0, rows // CONV_ROWS, body, 0)


def _mixer_body(x_ref, ha_ref, hb_ref, h0_ref, g_ref, win_ref, bin_ref, caw_ref, cab_ref,
                lng_ref, lnb_ref, wa_ref, cbw_ref, cbb_ref, lruw_ref, lrub_ref, lam_ref,
                wb_ref, wout_ref,
                o_ref, na_ref, nbuf_ref, nh_ref,
                abuf, bbuf, hbuf, conv_s, a_s, u_s, *, tt, nb):
    rows = tt * nb
    ha_rows = (CONV_A - 1) * nb
    hb_rows = (CONV_B - 1) * nb
    ti = pl.program_id(1)

    @pl.when(ti == 0)
    def _():
        abuf[0:ha_rows, :] = ha_ref[...].reshape(ha_rows, D)
        bbuf[0:hb_rows, :] = hb_ref[...].reshape(hb_rows, D)
        hbuf[...] = h0_ref[...]

    x = x_ref[...].reshape(rows, D)
    u = _rms(x, g_ref[0:1, :]).astype(BF16)

    def proj(s):
        return _dot(u, win_ref[s]) + bin_ref[s:s + 1, :]

    a_glu = proj(0) * _sigmoid(proj(1))
    abuf[ha_rows:ha_rows + rows, :] = a_glu
    _causal_conv(abuf, caw_ref, cab_ref, conv_s, CONV_A, rows, nb)
    abuf[0:ha_rows, :] = abuf[rows:rows + ha_rows, :]
    ac = conv_s[...]
    mu = jnp.mean(ac, axis=-1, keepdims=True)
    acc = ac - mu
    var = jnp.mean(acc * acc, axis=-1, keepdims=True)
    ln = acc * lax.rsqrt(var + EPS) * lng_ref[...] + lnb_ref[...]
    y_a = _dot((ln * _sigmoid(ln)).astype(BF16), wa_ref[...])

    bbuf[hb_rows:hb_rows + rows, :] = proj(2)
    _causal_conv(bbuf, cbw_ref, cbb_ref, conv_s, CONV_B, rows, nb)
    bbuf[0:hb_rows, :] = bbuf[rows:rows + hb_rows, :]
    xb = conv_s[...]
    xb16 = xb.astype(BF16)
    ri = [_dot(xb16[:, n * LRU_BLOCK:(n + 1) * LRU_BLOCK], lruw_ref[n]) for n in range(LRU_BLOCKS)]
    r_pre = jnp.concatenate([p[:, :LRU_BLOCK] for p in ri], axis=-1)
    i_pre = jnp.concatenate([p[:, LRU_BLOCK:] for p in ri], axis=-1)
    r = _sigmoid(r_pre + lrub_ref[0:1, :])
    gi = _sigmoid(i_pre + lrub_ref[1:2, :])
    nlam = -lam_ref[...]
    softplus = jnp.maximum(nlam, 0.0) + jnp.log1p(jnp.exp(-jnp.abs(nlam)))
    log_a = (-LRU_C) * r * softplus
    a = jnp.exp(log_a)
    one_m_a2 = -jnp.tanh(log_a) * (a * a + 1.0)
    a_s[...] = a
    u_s[...] = jnp.sqrt(one_m_a2) * (gi * xb)

    def scan_step(t, h):
        r0 = pl.multiple_of(t * nb, nb)
        h = a_s[pl.ds(r0, nb), :] * h + u_s[pl.ds(r0, nb), :]
        u_s[pl.ds(r0, nb), :] = h
        return h

    h_last = lax.fori_loop(0, tt, scan_step, hbuf[...])
    hbuf[...] = h_last
    hs = u_s[...]
    bg = proj(3)
    gelu = 0.5 * bg * (1.0 + jnp.tanh(math.sqrt(2.0 / math.pi) * (bg + 0.044715 * (bg * bg * bg))))
    y_b = _dot((hs * gelu).astype(BF16), wb_ref[...])

    merged = _sigmoid(proj(4)) * y_a + _sigmoid(proj(5)) * y_b
    m = _dot(merged.astype(BF16), wout_ref[...])
    o_ref[...] = (x + _rms(m, g_ref[1:2, :])).reshape(tt, nb, D)

    @pl.when(ti == pl.num_programs(1) - 1)
    def _():
        na_ref[...] = abuf[0:ha_rows, :].reshape(CONV_A - 1, nb, D)
        nbuf_ref[...] = bbuf[0:hb_rows, :].reshape(CONV_B - 1, nb, D)
        nh_ref[...] = h_last


def _mixer(x3, ha, hb, h0, w, tt, nb):
    n_t, n_b, _ = x3.shape
    rows = tt * nb
    grid = (n_b // nb, n_t // tt)
    x_spec = pl.BlockSpec((tt, nb, D), lambda j, i: (i, j, 0))

    def state_spec(n):
        return pl.BlockSpec((n, nb, D), lambda j, i: (0, j, 0))

    h_spec = pl.BlockSpec((nb, D), lambda j, i: (j, 0))
    weights = [w["g"], w["win"], w["bin"], w["caw"], w["cab"], w["lng"], w["lnb"], w["wa"],
               w["cbw"], w["cbb"], w["lruw"], w["lrub"], w["lam"], w["wb"], w["wout"]]
    return pl.pallas_call(
        functools.partial(_mixer_body, tt=tt, nb=nb),
        grid=grid,
        in_specs=[x_spec, state_spec(CONV_A - 1), state_spec(CONV_B - 1), h_spec]
                 + [_resident(a.shape) for a in weights],
        out_specs=[x_spec, state_spec(CONV_A - 1), state_spec(CONV_B - 1), h_spec],
        out_shape=[jax.ShapeDtypeStruct(x3.shape, F32),
                   jax.ShapeDtypeStruct(ha.shape, F32),
                   jax.ShapeDtypeStruct(hb.shape, F32),
                   jax.ShapeDtypeStruct(h0.shape, F32)],
        scratch_shapes=[pltpu.VMEM(((CONV_A - 1) * nb + rows, D), F32),
                        pltpu.VMEM(((CONV_B - 1) * nb + rows, D), F32),
                        pltpu.VMEM((nb, D), F32),
                        pltpu.VMEM((rows, D), F32),
                        pltpu.VMEM((rows, D), F32),
                        pltpu.VMEM((rows, D), F32)],
        compiler_params=_params(("arbitrary", "arbitrary")),
        name="mixer",
    )(x3, ha, hb, h0, *weights)


def _layer_weights(l, w_in, b_in, conv_a_w, conv_a_b, conv_ln_g, conv_ln_b, w_a_out, conv_b_w,
                   conv_b_b, lru_w_a, lru_b_a, lru_w_x, lru_b_x, lru_lambda, w_b_out, w_out,
                   xa_w_q, xa_w_o, ffn1_w_in, ffn1_w_out, ffn2_w_in, ffn2_w_out, norm_g):
    g = norm_g[l]

    def ffn_w(w_in_l, w_out_l):
        def chunks(m):
            return m.reshape(D, N_FF_CHUNKS, FF_CHUNK).transpose(1, 0, 2).astype(BF16)
        return (chunks(w_in_l[:, :D_FF]), chunks(w_in_l[:, D_FF:]),
                w_out_l.reshape(N_FF_CHUNKS, FF_CHUNK, D).astype(BF16))

    mixer = dict(
        g=g[2:4],
        win=w_in[l].reshape(D, 6, D).transpose(1, 0, 2).astype(BF16),
        bin=b_in[l].reshape(6, D),
        caw=conv_a_w[l], cab=conv_a_b[l][None], lng=conv_ln_g[l][None], lnb=conv_ln_b[l][None],
        wa=w_a_out[l].astype(BF16),
        cbw=conv_b_w[l], cbb=conv_b_b[l][None],
        lruw=jnp.concatenate([lru_w_a[l], lru_w_x[l]], axis=-1).astype(BF16),
        lrub=jnp.stack([lru_b_a[l], lru_b_x[l]]),
        lam=lru_lambda[l][None],
        wb=w_b_out[l].astype(BF16), wout=w_out[l].astype(BF16))
    return dict(
        ffn1=(g[0:2],) + ffn_w(ffn1_w_in[l], ffn1_w_out[l]),
        ffn2=(g[7:9],) + ffn_w(ffn2_w_in[l], ffn2_w_out[l]),
        mixer=mixer,
        xa=(g[4:6], xa_w_q[l].reshape(D, D).astype(BF16), xa_w_o[l].reshape(D, D).astype(BF16)))


def kernel(x_prompt, x_sample, mem_prompt, cache_mem_k, cache_mem_v, state_conv_a, state_conv_b, state_lru_h, w_in, b_in, conv_a_w, conv_a_b, conv_ln_g, conv_ln_b, w_a_out, conv_b_w, conv_b_b, lru_w_a, lru_b_a, lru_w_x, lru_b_x, lru_lambda, w_b_out, w_out, xa_w_q, xa_w_k, xa_w_v, xa_w_o, ffn1_w_in, ffn1_w_out, ffn2_w_in, ffn2_w_out, norm_g):
    n_b, n_t, _ = x_prompt.shape
    s_b, s_t, _ = x_sample.shape
    layers = [_layer_weights(l, w_in, b_in, conv_a_w, conv_a_b, conv_ln_g, conv_ln_b, w_a_out,
                             conv_b_w, conv_b_b, lru_w_a, lru_b_a, lru_w_x, lru_b_x, lru_lambda,
                             w_b_out, w_out, xa_w_q, xa_w_o, ffn1_w_in, ffn1_w_out, ffn2_w_in,
                             ffn2_w_out, norm_g) for l in range(DEPTH)]

    mem_k, mem_v = _memkv(mem_prompt.reshape(n_b * N_MEM, D), norm_g[:, 6:7, :],
                          xa_w_k.reshape(DEPTH, D, D).astype(BF16),
                          xa_w_v.reshape(DEPTH, D, D).astype(BF16))

    tm = 512
    bt_spec = pl.BlockSpec((None, tm, D), lambda b, i: (b, i, 0))
    tb_spec = pl.BlockSpec((tm, D), lambda b, i: (i, b))
    row_spec = pl.BlockSpec((tm, D), lambda i: (i, 0))
    zero_a = jnp.zeros((CONV_A - 1, n_b, D), F32)
    zero_b = jnp.zeros((CONV_B - 1, n_b, D), F32)
    zero_h = jnp.zeros((n_b, D), F32)
    p_a, p_b, p_h = [], [], []
    x = x_prompt
    for l, lw in enumerate(layers):
        if l == 0:
            x = _ffn(x, bt_spec, (n_t, n_b * D), tb_spec, (n_b, n_t // tm), *lw["ffn1"])
        else:
            x = _ffn(x.reshape(n_t * n_b, D), row_spec, (n_t * n_b, D), row_spec,
                     (n_t * n_b // tm,), *lw["ffn1"])
        x, na, nbuf, nh = _mixer(x.reshape(n_t, n_b, D), zero_a, zero_b, zero_h, lw["mixer"],
                                 tt=32, nb=n_b)
        p_a.append(na.transpose(1, 0, 2))
        p_b.append(nbuf.transpose(1, 0, 2))
        p_h.append(nh)
        x = _xattn_prompt(x.reshape(n_t, n_b * D), mem_k[l], mem_v[l], *lw["xa"], n_batch=n_b)
        if l == DEPTH - 1:
            x = _ffn(x, tb_spec, (n_b, n_t, D), bt_spec, (n_b, n_t // tm), *lw["ffn2"])
        else:
            x = _ffn(x.reshape(n_t * n_b, D), row_spec, (n_t * n_b, D), row_spec,
                     (n_t * n_b // tm,), *lw["ffn2"])
    y_prompt = x

    ts = 256
    srow_spec = pl.BlockSpec((ts, D), lambda i: (i, 0))
    s_a, s_b_, s_h = [], [], []
    x = x_sample.reshape(s_b * s_t, D)
    for l, lw in enumerate(layers):
        x = _ffn(x, srow_spec, (s_b * s_t, D), srow_spec, (s_b * s_t // ts,), *lw["ffn1"])
        x_tm = x.reshape(s_b, s_t, D).transpose(1, 0, 2)
        x_tm, na, nbuf, nh = _mixer(x_tm, state_conv_a[l].transpose(1, 0, 2),
                                    state_conv_b[l].transpose(1, 0, 2), state_lru_h[l],
                                    lw["mixer"], tt=s_t, nb=32)
        s_a.append(na.transpose(1, 0, 2))
        s_b_.append(nbuf.transpose(1, 0, 2))
        s_h.append(nh)
        x = x_tm.transpose(1, 0, 2).reshape(s_b * s_t, D)
        x = _xattn_sample(x, cache_mem_k[l].reshape(s_b, N_MEM, D),
                          cache_mem_v[l].reshape(s_b, N_MEM, D), *lw["xa"], tq=s_t)
        x = _ffn(x, srow_spec, (s_b * s_t, D), srow_spec, (s_b * s_t // ts,), *lw["ffn2"])
    y_sample = x.reshape(s_b, s_t, D)

    kv_shape = (DEPTH, n_b, N_MEM, HEADS, HEAD_DIM)
    return (y_prompt, y_sample, mem_k.reshape(kv_shape), mem_v.reshape(kv_shape),
            jnp.stack(p_a), jnp.stack(p_b), jnp.stack(p_h),
            jnp.stack(s_a), jnp.stack(s_b_), jnp.stack(s_h))
```

```python
import functools
import math

import jax
import jax.numpy as jnp
from jax import lax
from jax.experimental import pallas as pl
from jax.experimental.pallas import tpu as pltpu

D = 1024
DEPTH = 2
N_MEM = 256
HEADS = 4
HEAD_DIM = D // HEADS
CONV_A = 31
CONV_B = 4
LRU_BLOCKS = 8
LRU_BLOCK = D // LRU_BLOCKS
LRU_C = 8.0
D_FF = 2816
EPS = 1e-6
G_FFN1, G_MIX, G_XA, G_FFN2 = 0, 2, 4, 7

LANES = 128
SUBLANES = 8
MXU_N = 256
FF_CHUNK = MXU_N
N_FF_CHUNKS = D_FF // FF_CHUNK
VMEM_LIMIT = 56 * 1024 * 1024

F32 = jnp.float32
BF16 = jnp.bfloat16


def _dot(a, b):
    return jnp.dot(a, b, preferred_element_type=F32)


def _rms(x, g):
    ms = jnp.mean(x * x, axis=-1, keepdims=True)
    return x * lax.rsqrt(ms + EPS) * g


def _sigmoid(x):
    return 0.5 + 0.5 * jnp.tanh(0.5 * x)


def _layer(shape, l):
    nd = len(shape)
    return pl.BlockSpec((None,) + tuple(shape), lambda *_: (l,) + (0,) * nd,
                        pipeline_mode=pl.Buffered(1))


def _params(sem):
    return pltpu.CompilerParams(dimension_semantics=sem, vmem_limit_bytes=VMEM_LIMIT)


def _to_time_major(v, nb, tt):
    return jnp.swapaxes(v.reshape(nb, tt, D), 0, 1).reshape(tt * nb, D)


def _to_batch_major(v, nb, tt):
    return jnp.swapaxes(v.reshape(tt, nb, D), 0, 1).reshape(nb * tt, D)


def _ffn_body(x_ref, g_ref, win_ref, wout_ref, o_ref, *, g_row, relayout):
    if relayout == "in":
        nb, tq, _ = x_ref.shape
        x = x_ref[...].reshape(nb * tq, D)
    else:
        x = x_ref[...]
    xn = _rms(x, g_ref[g_row:g_row + 1, :]).astype(BF16)
    acc = None
    for j in range(N_FF_CHUNKS):
        lo = j * FF_CHUNK
        gate = _dot(xn, win_ref[:, lo:lo + FF_CHUNK])
        up = _dot(xn, win_ref[:, D_FF + lo:D_FF + lo + FF_CHUNK])
        h = (gate * _sigmoid(gate) * up).astype(BF16)
        d = _dot(h, wout_ref[lo:lo + FF_CHUNK, :])
        acc = d if acc is None else acc + d
    y = x + 0.5 * _rms(acc, g_ref[g_row + 1:g_row + 2, :])
    if relayout == "in":
        o_ref[...] = _to_time_major(y, nb, tq)
    elif relayout == "out":
        nb, tq, _ = o_ref.shape
        o_ref[...] = _to_batch_major(y, nb, tq).reshape(nb, tq, D)
    else:
        o_ref[...] = y


def _ffn(x, l, g_row, norm_g, w_in16, w_out16, tm, relayout=None, n_batch=None):
    if relayout == "in":
        n_b, n_t, _ = x.shape
        tq = tm // n_b
        grid = (n_t // tq,)
        x_spec = pl.BlockSpec((n_b, tq, D), lambda i: (0, i, 0))
        o_spec = pl.BlockSpec((tm, D), lambda i: (i, 0))
        out_shape = (n_t * n_b, D)
    elif relayout == "out":
        rows = x.shape[0]
        tq = tm // n_batch
        grid = (rows // tm,)
        x_spec = pl.BlockSpec((tm, D), lambda i: (i, 0))
        o_spec = pl.BlockSpec((n_batch, tq, D), lambda i: (0, i, 0))
        out_shape = (n_batch, rows // n_batch, D)
    else:
        rows = x.shape[0]
        grid = (rows // tm,)
        x_spec = o_spec = pl.BlockSpec((tm, D), lambda i: (i, 0))
        out_shape = (rows, D)
    return pl.pallas_call(
        functools.partial(_ffn_body, g_row=g_row, relayout=relayout),
        grid=grid,
        in_specs=[x_spec, _layer(norm_g.shape[1:], l), _layer(w_in16.shape[1:], l),
                  _layer(w_out16.shape[1:], l)],
        out_specs=o_spec,
        out_shape=jax.ShapeDtypeStruct(out_shape, F32),
        compiler_params=_params(("arbitrary",)),
        name="ffn",
    )(x, norm_g, w_in16, w_out16)


def _memkv_body(m_ref, g_ref, wk_ref, wv_ref, k_ref, v_ref, k16_ref, v16_ref):
    mn = _rms(m_ref[...], g_ref[6:7, :]).astype(BF16)
    k = _dot(mn, wk_ref[...])
    v = _dot(mn, wv_ref[...])
    k_ref[...] = k
    v_ref[...] = v
    k16_ref[...] = k.astype(BF16)
    v16_ref[...] = v.astype(BF16)


def _memkv(mem2d, norm_g, wk16, wv16, tm=512):
    rows = mem2d.shape[0]
    w_spec = pl.BlockSpec((None, D, D), lambda l, i: (l, 0, 0))
    o_spec = pl.BlockSpec((None, tm, D), lambda l, i: (l, i, 0))
    return pl.pallas_call(
        _memkv_body,
        grid=(DEPTH, rows // tm),
        in_specs=[pl.BlockSpec((tm, D), lambda l, i: (i, 0)),
                  pl.BlockSpec((None,) + norm_g.shape[1:], lambda l, i: (l, 0, 0)),
                  w_spec, w_spec],
        out_specs=[o_spec] * 4,
        out_shape=[jax.ShapeDtypeStruct((DEPTH, rows, D), F32)] * 2
                  + [jax.ShapeDtypeStruct((DEPTH, rows, D), BF16)] * 2,
        compiler_params=_params(("arbitrary", "arbitrary")),
        name="memkv",
    )(mem2d, norm_g, wk16, wv16)


def _softmax_rows(s):
    e = jnp.exp(s - jnp.max(s, axis=-1, keepdims=True))
    return e * (1.0 / jnp.sum(e, axis=-1, keepdims=True))


def _xattn_prompt_body(x_ref, k_ref, v_ref, g_ref, wq_ref, wo_ref, o_ref, *, nb):
    rows = x_ref.shape[0]
    tt = rows // nb
    x = x_ref[...]
    u = _rms(x, g_ref[G_XA:G_XA + 1, :]).astype(BF16)
    q = _to_batch_major(_dot(u, wq_ref[...]) * (HEAD_DIM ** -0.5), nb, tt)
    outs = []
    for b in range(nb):
        qb = q[b * tt:(b + 1) * tt].astype(BF16)
        heads = []
        for h in range(HEADS):
            hs = slice(h * HEAD_DIM, (h + 1) * HEAD_DIM)
            s = lax.dot_general(qb[:, hs], k_ref[b * N_MEM:(b + 1) * N_MEM, hs],
                                (((1,), (1,)), ((), ())), preferred_element_type=F32)
            heads.append(_dot(_softmax_rows(s).astype(BF16), v_ref[b * N_MEM:(b + 1) * N_MEM, hs]))
        outs.append(jnp.concatenate(heads, axis=-1))
    o = _to_time_major(jnp.concatenate(outs, axis=0), nb, tt)
    y = _dot(o.astype(BF16), wo_ref[...])
    o_ref[...] = x + _rms(y, g_ref[G_XA + 1:G_XA + 2, :])


def _xattn_prompt(x, l, norm_g, k16, v16, wq16, wo16, nb, tt):
    rows = tt * nb
    x_spec = pl.BlockSpec((rows, D), lambda i: (i, 0))
    return pl.pallas_call(
        functools.partial(_xattn_prompt_body, nb=nb),
        grid=(x.shape[0] // rows,),
        in_specs=[x_spec, _layer(k16.shape[1:], l), _layer(v16.shape[1:], l),
                  _layer(norm_g.shape[1:], l), _layer(wq16.shape[1:], l), _layer(wo16.shape[1:], l)],
        out_specs=x_spec,
        out_shape=jax.ShapeDtypeStruct(x.shape, F32),
        compiler_params=_params(("arbitrary",)),
        name="xattn_prompt",
    )(x, k16, v16, norm_g, wq16, wo16)


def _xattn_sample_body(x_ref, k_ref, v_ref, g_ref, wq_ref, wo_ref, o_ref, *, bb, tq):
    x = x_ref[...]
    u = _rms(x, g_ref[G_XA:G_XA + 1, :]).astype(BF16)
    q = _dot(u, wq_ref[...]) * (HEAD_DIM ** -0.5)
    n_q, n_k = HEADS * tq, N_MEM * HEADS
    row_head = lax.broadcasted_iota(jnp.int32, (n_q, n_k), 0) // tq
    col_head = lax.broadcasted_iota(jnp.int32, (n_q, n_k), 1) % HEADS
    own_head = row_head == col_head
    outs = []
    for b in range(bb):
        qb = q[b * tq:(b + 1) * tq]
        q2 = jnp.concatenate([qb[:, h * HEAD_DIM:(h + 1) * HEAD_DIM] for h in range(HEADS)], axis=0)
        k2 = k_ref[b].reshape(n_k, HEAD_DIM)
        v2 = v_ref[b].reshape(n_k, HEAD_DIM)
        s = lax.dot_general(q2, k2, (((1,), (1,)), ((), ())), preferred_element_type=F32)
        p = _softmax_rows(jnp.where(own_head, s, jnp.finfo(F32).min))
        o2 = _dot(p, v2)
        outs.append(jnp.concatenate([o2[h * tq:(h + 1) * tq] for h in range(HEADS)], axis=-1))
    o = jnp.concatenate(outs, axis=0)
    y = _dot(o.astype(BF16), wo_ref[...])
    o_ref[...] = x + _rms(y, g_ref[G_XA + 1:G_XA + 2, :])


def _xattn_sample(x, l, norm_g, cache_k, cache_v, wq16, wo16, tq, bb):
    n_batch = cache_k.shape[1]
    x_spec = pl.BlockSpec((bb * tq, D), lambda i: (i, 0))
    kv_spec = pl.BlockSpec((None, bb, N_MEM, HEADS, HEAD_DIM), lambda i: (l, i, 0, 0, 0))
    return pl.pallas_call(
        functools.partial(_xattn_sample_body, bb=bb, tq=tq),
        grid=(n_batch // bb,),
        in_specs=[x_spec, kv_spec, kv_spec, _layer(norm_g.shape[1:], l), _layer(wq16.shape[1:], l),
                  _layer(wo16.shape[1:], l)],
        out_specs=x_spec,
        out_shape=jax.ShapeDtypeStruct(x.shape, F32),
        compiler_params=_params(("arbitrary",)),
        name="xattn_sample",
    )(x, cache_k, cache_v, norm_g, wq16, wo16)


FIR_STEPS = SUBLANES


def _causal_conv(buf_ref, w_ref, b_ref, out_ref, n_taps, tt, nb):
    groups = nb // SUBLANES
    n_in = FIR_STEPS + n_taps - 1
    for c in range(D // LANES):
        cs = slice(c * LANES, (c + 1) * LANES)
        wk = [jnp.broadcast_to(w_ref[k:k + 1, cs], (SUBLANES, LANES)) for k in range(n_taps)]
        bias = jnp.broadcast_to(b_ref[0:1, cs], (SUBLANES, LANES))

        def body(i, carry):
            base = (i // groups) * (FIR_STEPS * nb) + (i % groups) * SUBLANES
            base = pl.multiple_of(base, SUBLANES)
            acc = [bias] * FIR_STEPS
            for j in range(n_in):
                xj = buf_ref[pl.ds(base + j * nb, SUBLANES), cs]
                for o in range(FIR_STEPS):
                    if 0 <= j - o < n_taps:
                        acc[o] = acc[o] + wk[j - o] * xj
            for o in range(FIR_STEPS):
                out_ref[pl.ds(base + o * nb, SUBLANES), cs] = acc[o]
            return carry

        lax.fori_loop(0, (tt // FIR_STEPS) * groups, body, 0)


def _mixer_body(*refs, tt, nb, has_state):
    if has_state:
        x_ref, ha_ref, hb_ref, h0_ref = refs[:4]
        refs = refs[4:]
    else:
        x_ref = refs[0]
        refs = refs[1:]
    (g_ref, win_ref, bin_ref, caw_ref, cab_ref, lng_ref, lnb_ref, wa_ref, cbw_ref, cbb_ref,
     lruw_ref, lba_ref, lbx_ref, lam_ref, wb_ref, wout_ref,
     o_ref, na_ref, nbuf_ref, nh_ref,
     abuf, bbuf, hbuf, conv_s, a_s, u_s) = refs
    rows = tt * nb
    ha_rows = (CONV_A - 1) * nb
    hb_rows = (CONV_B - 1) * nb
    ti = pl.program_id(1)

    @pl.when(ti == 0)
    def _():
        if has_state:
            abuf[0:ha_rows, :] = jnp.swapaxes(ha_ref[...], 0, 1).reshape(ha_rows, D)
            bbuf[0:hb_rows, :] = jnp.swapaxes(hb_ref[...], 0, 1).reshape(hb_rows, D)
            hbuf[...] = h0_ref[...]
        else:
            abuf[0:ha_rows, :] = jnp.zeros((ha_rows, D), F32)
            bbuf[0:hb_rows, :] = jnp.zeros((hb_rows, D), F32)
            hbuf[...] = jnp.zeros((nb, D), F32)

    if has_state:
        x = _to_time_major(x_ref[...].reshape(rows, D), nb, tt)
    else:
        x = x_ref[...]
    u = _rms(x, g_ref[G_MIX:G_MIX + 1, :]).astype(BF16)

    def proj(s):
        return _dot(u, win_ref[:, s * D:(s + 1) * D]) + bin_ref[:, s * D:(s + 1) * D]

    abuf[ha_rows:ha_rows + rows, :] = proj(0) * _sigmoid(proj(1))
    _causal_conv(abuf, caw_ref, cab_ref, conv_s, CONV_A, tt, nb)
    abuf[0:ha_rows, :] = abuf[rows:rows + ha_rows, :]
    ac = conv_s[...]
    mu = jnp.mean(ac, axis=-1, keepdims=True)
    acc = ac - mu
    var = jnp.mean(acc * acc, axis=-1, keepdims=True)
    ln = acc * lax.rsqrt(var + EPS) * lng_ref[...] + lnb_ref[...]
    y_a = _dot((ln * _sigmoid(ln)).astype(BF16), wa_ref[...])

    bbuf[hb_rows:hb_rows + rows, :] = proj(2)
    _causal_conv(bbuf, cbw_ref, cbb_ref, conv_s, CONV_B, tt, nb)
    bbuf[0:hb_rows, :] = bbuf[rows:rows + hb_rows, :]
    xb = conv_s[...]
    xb16 = xb.astype(BF16)
    ri = [_dot(xb16[:, n * LRU_BLOCK:(n + 1) * LRU_BLOCK], lruw_ref[n]) for n in range(LRU_BLOCKS)]
    r_pre = jnp.concatenate([p[:, :LRU_BLOCK] for p in ri], axis=-1)
    i_pre = jnp.concatenate([p[:, LRU_BLOCK:] for p in ri], axis=-1)
    r = _sigmoid(r_pre + lba_ref[...])
    gi = _sigmoid(i_pre + lbx_ref[...])
    nlam = -lam_ref[...]
    softplus = jnp.maximum(nlam, 0.0) + jnp.log1p(jnp.exp(-jnp.abs(nlam)))
    log_a = (-LRU_C) * r * softplus
    a = jnp.exp(log_a)
    one_m_a2 = -jnp.tanh(log_a) * (a * a + 1.0)
    a_s[...] = a
    u_s[...] = jnp.sqrt(one_m_a2) * (gi * xb)

    def scan_step(t, h):
        r0 = pl.multiple_of(t * nb, nb)
        h = a_s[pl.ds(r0, nb), :] * h + u_s[pl.ds(r0, nb), :]
        u_s[pl.ds(r0, nb), :] = h
        return h

    h_last = lax.fori_loop(0, tt, scan_step, hbuf[...])
    hbuf[...] = h_last
    hs = u_s[...]
    bg = proj(3)
    gelu = 0.5 * bg * (1.0 + jnp.tanh(math.sqrt(2.0 / math.pi) * (bg + 0.044715 * (bg * bg * bg))))
    y_b = _dot((hs * gelu).astype(BF16), wb_ref[...])

    merged = _sigmoid(proj(4)) * y_a + _sigmoid(proj(5)) * y_b
    m = _dot(merged.astype(BF16), wout_ref[...])
    y = x + _rms(m, g_ref[G_MIX + 1:G_MIX + 2, :])
    if has_state:
        o_ref[...] = _to_batch_major(y, nb, tt).reshape(nb, tt, D)
    else:
        o_ref[...] = y

    @pl.when(ti == pl.num_programs(1) - 1)
    def _():
        na_ref[...] = jnp.swapaxes(abuf[0:ha_rows, :].reshape(CONV_A - 1, nb, D), 0, 1)
        nbuf_ref[...] = jnp.swapaxes(bbuf[0:hb_rows, :].reshape(CONV_B - 1, nb, D), 0, 1)
        nh_ref[...] = h_last


def _mixer(x, state, l, w, tt, nb):
    has_state = state is not None
    rows = tt * nb
    if has_state:
        n_b = x.shape[0]
        grid = (n_b // nb, 1)
        x_spec = pl.BlockSpec((nb, tt, D), lambda j, i: (j, 0, 0))
    else:
        n_b = nb
        grid = (1, x.shape[0] // rows)
        x_spec = pl.BlockSpec((rows, D), lambda j, i: (i, 0))

    def state_spec(n):
        return pl.BlockSpec((nb, n, D), lambda j, i: (j, 0, 0))

    h_spec = pl.BlockSpec((nb, D), lambda j, i: (j, 0))
    weights = [w["norm_g"], w["w_in"], w["b_in"], w["conv_a_w"], w["conv_a_b"], w["conv_ln_g"],
               w["conv_ln_b"], w["w_a_out"], w["conv_b_w"], w["conv_b_b"], w["lru_w"], w["lru_b_a"],
               w["lru_b_x"], w["lru_lambda"], w["w_b_out"], w["w_out"]]
    state_in = list(state) if has_state else []
    state_specs = [state_spec(CONV_A - 1), state_spec(CONV_B - 1), h_spec]
    state_in_specs = [pl.BlockSpec((None, nb, CONV_A - 1, D), lambda j, i: (l, j, 0, 0)),
                      pl.BlockSpec((None, nb, CONV_B - 1, D), lambda j, i: (l, j, 0, 0)),
                      pl.BlockSpec((None, nb, D), lambda j, i: (l, j, 0))]
    return pl.pallas_call(
        functools.partial(_mixer_body, tt=tt, nb=nb, has_state=has_state),
        grid=grid,
        in_specs=[x_spec] + (state_in_specs if has_state else [])
                 + [_layer(a.shape[1:], l) for a in weights],
        out_specs=[x_spec] + state_specs,
        out_shape=[jax.ShapeDtypeStruct(x.shape, F32),
                   jax.ShapeDtypeStruct((n_b, CONV_A - 1, D), F32),
                   jax.ShapeDtypeStruct((n_b, CONV_B - 1, D), F32),
                   jax.ShapeDtypeStruct((n_b, D), F32)],
        scratch_shapes=[pltpu.VMEM(((CONV_A - 1) * nb + rows, D), F32),
                        pltpu.VMEM(((CONV_B - 1) * nb + rows, D), F32),
                        pltpu.VMEM((nb, D), F32),
                        pltpu.VMEM((rows, D), F32),
                        pltpu.VMEM((rows, D), F32),
                        pltpu.VMEM((rows, D), F32)],
        compiler_params=_params(("arbitrary", "arbitrary")),
        name="mixer",
    )(x, *state_in, *weights)


def kernel(x_prompt, x_sample, mem_prompt, cache_mem_k, cache_mem_v, state_conv_a, state_conv_b, state_lru_h, w_in, b_in, conv_a_w, conv_a_b, conv_ln_g, conv_ln_b, w_a_out, conv_b_w, conv_b_b, lru_w_a, lru_b_a, lru_w_x, lru_b_x, lru_lambda, w_b_out, w_out, xa_w_q, xa_w_k, xa_w_v, xa_w_o, ffn1_w_in, ffn1_w_out, ffn2_w_in, ffn2_w_out, norm_g):
    n_b, n_t, _ = x_prompt.shape
    s_b, s_t, _ = x_sample.shape

    def vec(a):
        return a.reshape(DEPTH, 1, a.shape[-1])

    mixer_w = dict(
        norm_g=norm_g, w_in=w_in.astype(BF16), b_in=vec(b_in), conv_a_w=conv_a_w,
        conv_a_b=vec(conv_a_b), conv_ln_g=vec(conv_ln_g), conv_ln_b=vec(conv_ln_b),
        w_a_out=w_a_out.astype(BF16), conv_b_w=conv_b_w, conv_b_b=vec(conv_b_b),
        lru_w=jnp.concatenate([lru_w_a, lru_w_x], axis=-1).astype(BF16),
        lru_b_a=vec(lru_b_a), lru_b_x=vec(lru_b_x), lru_lambda=vec(lru_lambda),
        w_b_out=w_b_out.astype(BF16), w_out=w_out.astype(BF16))
    ffn_w = {G_FFN1: (ffn1_w_in.astype(BF16), ffn1_w_out.astype(BF16)),
             G_FFN2: (ffn2_w_in.astype(BF16), ffn2_w_out.astype(BF16))}
    wq16 = xa_w_q.reshape(DEPTH, D, D).astype(BF16)
    wo16 = xa_w_o.reshape(DEPTH, D, D).astype(BF16)

    mem_k, mem_v, k16, v16 = _memkv(mem_prompt.reshape(n_b * N_MEM, D), norm_g,
                                    xa_w_k.reshape(DEPTH, D, D).astype(BF16),
                                    xa_w_v.reshape(DEPTH, D, D).astype(BF16))

    tm = 512
    p_a, p_b, p_h = [], [], []
    x = x_prompt
    for l in range(DEPTH):
        x = _ffn(x, l, G_FFN1, norm_g, *ffn_w[G_FFN1], tm=tm, relayout="in" if l == 0 else None)
        x, na, nbuf, nh = _mixer(x, None, l, mixer_w, tt=32, nb=n_b)
        p_a.append(na)
        p_b.append(nbuf)
        p_h.append(nh)
        x = _xattn_prompt(x, l, norm_g, k16, v16, wq16, wo16, nb=n_b, tt=128)
        x = _ffn(x, l, G_FFN2, norm_g, *ffn_w[G_FFN2], tm=tm,
                 relayout="out" if l == DEPTH - 1 else None, n_batch=n_b)
    y_prompt = x

    ts = 256
    s_a, s_bb, s_h = [], [], []
    x = x_sample.reshape(s_b * s_t, D)
    for l in range(DEPTH):
        x = _ffn(x, l, G_FFN1, norm_g, *ffn_w[G_FFN1], tm=ts)
        x, na, nbuf, nh = _mixer(x.reshape(s_b, s_t, D),
                                 (state_conv_a, state_conv_b, state_lru_h),
                                 l, mixer_w, tt=s_t, nb=32)
        s_a.append(na)
        s_bb.append(nbuf)
        s_h.append(nh)
        x = _xattn_sample(x.reshape(s_b * s_t, D), l, norm_g, cache_mem_k, cache_mem_v, wq16, wo16,
                          tq=s_t, bb=8)
        x = _ffn(x, l, G_FFN2, norm_g, *ffn_w[G_FFN2], tm=ts)
    y_sample = x.reshape(s_b, s_t, D)

    kv_shape = (DEPTH, n_b, N_MEM, HEADS, HEAD_DIM)
    return (y_prompt, y_sample, mem_k.reshape(kv_shape), mem_v.reshape(kv_shape),
            jnp.stack(p_a), jnp.stack(p_b), jnp.stack(p_h),
            jnp.stack(s_a), jnp.stack(s_bb), jnp.stack(s_h))
```

```python
import functools
import math

import jax
import jax.numpy as jnp
from jax import lax
from jax.experimental import pallas as pl
from jax.experimental.pallas import tpu as pltpu

D = 1024
DEPTH = 2
N_MEM = 256
HEADS = 4
HEAD_DIM = D // HEADS
CONV_A = 31
CONV_B = 4
LRU_BLOCKS = 8
LRU_BLOCK = D // LRU_BLOCKS
LRU_C = 8.0
D_FF = 2816
EPS = 1e-6
G_FFN1, G_MIX, G_XA, G_FFN2 = 0, 2, 4, 7

LANES = 128
SUBLANES = 8
MXU_N = 256
FF_CHUNK = MXU_N
N_FF_CHUNKS = D_FF // FF_CHUNK
VMEM_LIMIT = 56 * 1024 * 1024

F32 = jnp.float32
BF16 = jnp.bfloat16


def _dot(a, b):
    return jnp.dot(a, b, preferred_element_type=F32)


def _dot_cols(a, w_ref, col0, n_cols):
    return jnp.concatenate(
        [_dot(a, w_ref[:, col0 + j:col0 + j + MXU_N]) for j in range(0, n_cols, MXU_N)], axis=-1)


def _rms(x, g):
    ms = jnp.mean(x * x, axis=-1, keepdims=True)
    return x * lax.rsqrt(ms + EPS) * g


def _sigmoid(x):
    return 0.5 + 0.5 * jnp.tanh(0.5 * x)


def _layer(shape, l):
    nd = len(shape)
    return pl.BlockSpec((None,) + tuple(shape), lambda *_: (l,) + (0,) * nd,
                        pipeline_mode=pl.Buffered(1))


def _params(sem):
    return pltpu.CompilerParams(dimension_semantics=sem, vmem_limit_bytes=VMEM_LIMIT)


def _to_time_major(v, nb, tt):
    return jnp.swapaxes(v.reshape(nb, tt, D), 0, 1).reshape(tt * nb, D)


def _to_batch_major(v, nb, tt):
    return jnp.swapaxes(v.reshape(tt, nb, D), 0, 1).reshape(nb * tt, D)


def _ffn_body(x_ref, g_ref, win_ref, wout_ref, o_ref, *, g_row, relayout):
    if relayout == "in":
        nb, tq, _ = x_ref.shape
        x = x_ref[...].reshape(nb * tq, D)
    else:
        x = x_ref[...]
    xn = _rms(x, g_ref[g_row:g_row + 1, :]).astype(BF16)
    acc = None
    for j in range(N_FF_CHUNKS):
        lo = j * FF_CHUNK
        gate = _dot(xn, win_ref[:, lo:lo + FF_CHUNK])
        up = _dot(xn, win_ref[:, D_FF + lo:D_FF + lo + FF_CHUNK])
        h = (gate * _sigmoid(gate) * up).astype(BF16)
        d = _dot(h, wout_ref[lo:lo + FF_CHUNK, :])
        acc = d if acc is None else acc + d
    y = x + 0.5 * _rms(acc, g_ref[g_row + 1:g_row + 2, :])
    if relayout == "in":
        o_ref[...] = _to_time_major(y, nb, tq)
    elif relayout == "out":
        nb, tq, _ = o_ref.shape
        o_ref[...] = _to_batch_major(y, nb, tq).reshape(nb, tq, D)
    else:
        o_ref[...] = y


def _ffn(x, l, g_row, norm_g, w_in16, w_out16, tm, relayout=None, n_batch=None):
    if relayout == "in":
        n_b, n_t, _ = x.shape
        tq = tm // n_b
        grid = (n_t // tq,)
        x_spec = pl.BlockSpec((n_b, tq, D), lambda i: (0, i, 0))
        o_spec = pl.BlockSpec((tm, D), lambda i: (i, 0))
        out_shape = (n_t * n_b, D)
    elif relayout == "out":
        rows = x.shape[0]
        tq = tm // n_batch
        grid = (rows // tm,)
        x_spec = pl.BlockSpec((tm, D), lambda i: (i, 0))
        o_spec = pl.BlockSpec((n_batch, tq, D), lambda i: (0, i, 0))
        out_shape = (n_batch, rows // n_batch, D)
    else:
        rows = x.shape[0]
        grid = (rows // tm,)
        x_spec = o_spec = pl.BlockSpec((tm, D), lambda i: (i, 0))
        out_shape = (rows, D)
    return pl.pallas_call(
        functools.partial(_ffn_body, g_row=g_row, relayout=relayout),
        grid=grid,
        in_specs=[x_spec, _layer(norm_g.shape[1:], l), _layer(w_in16.shape[1:], l),
                  _layer(w_out16.shape[1:], l)],
        out_specs=o_spec,
        out_shape=jax.ShapeDtypeStruct(out_shape, F32),
        compiler_params=_params(("arbitrary",)),
        name="ffn",
    )(x, norm_g, w_in16, w_out16)


def _memkv_body(m_ref, g_ref, wk_ref, wv_ref, k_ref, v_ref, k16_ref, v16_ref):
    mn = _rms(m_ref[...], g_ref[6:7, :]).astype(BF16)
    k = _dot(mn, wk_ref[...])
    v = _dot(mn, wv_ref[...])
    k_ref[...] = k
    v_ref[...] = v
    k16_ref[...] = k.astype(BF16)
    v16_ref[...] = v.astype(BF16)


def _memkv(mem2d, norm_g, wk16, wv16, tm=512):
    rows = mem2d.shape[0]
    w_spec = pl.BlockSpec((None, D, D), lambda l, i: (l, 0, 0))
    o_spec = pl.BlockSpec((None, tm, D), lambda l, i: (l, i, 0))
    return pl.pallas_call(
        _memkv_body,
        grid=(DEPTH, rows // tm),
        in_specs=[pl.BlockSpec((tm, D), lambda l, i: (i, 0)),
                  pl.BlockSpec((None,) + norm_g.shape[1:], lambda l, i: (l, 0, 0)),
                  w_spec, w_spec],
        out_specs=[o_spec] * 4,
        out_shape=[jax.ShapeDtypeStruct((DEPTH, rows, D), F32)] * 2
                  + [jax.ShapeDtypeStruct((DEPTH, rows, D), BF16)] * 2,
        compiler_params=_params(("arbitrary", "arbitrary")),
        name="memkv",
    )(mem2d, norm_g, wk16, wv16)


def _softmax_rows(s):
    e = jnp.exp(s - jnp.max(s, axis=-1, keepdims=True))
    return e * (1.0 / jnp.sum(e, axis=-1, keepdims=True))


def _xattn_prompt_body(x_ref, k_ref, v_ref, g_ref, wq_ref, wo_ref, o_ref, q_s, o_s, *, nb):
    rows = x_ref.shape[0]
    tt = rows // nb
    n_lb = D // LANES
    lb_per_head = HEAD_DIM // LANES
    x = x_ref[...]
    u = _rms(x, g_ref[G_XA:G_XA + 1, :]).astype(BF16)
    q = _dot(u, wq_ref[...]) * (HEAD_DIM ** -0.5)
    for c in range(n_lb):
        q_s[c] = q[:, c * LANES:(c + 1) * LANES]
    scores = []
    for b in range(nb):
        for h in range(HEADS):
            qbh = jnp.concatenate(
                [q_s[h * lb_per_head + i, pl.ds(b, tt, stride=nb), :] for i in range(lb_per_head)],
                axis=-1).astype(BF16)
            scores.append(lax.dot_general(
                qbh, k_ref[b * N_MEM:(b + 1) * N_MEM, h * HEAD_DIM:(h + 1) * HEAD_DIM],
                (((1,), (1,)), ((), ())), preferred_element_type=F32))
    p = _softmax_rows(jnp.stack(scores)).astype(BF16)
    for b in range(nb):
        for h in range(HEADS):
            obh = _dot(p[b * HEADS + h],
                       v_ref[b * N_MEM:(b + 1) * N_MEM, h * HEAD_DIM:(h + 1) * HEAD_DIM])
            for i in range(lb_per_head):
                o_s[h * lb_per_head + i, pl.ds(b, tt, stride=nb), :] = obh[:, i * LANES:(i + 1) * LANES]
    o = jnp.concatenate([o_s[c] for c in range(n_lb)], axis=-1)
    y = _dot(o.astype(BF16), wo_ref[...])
    o_ref[...] = x + _rms(y, g_ref[G_XA + 1:G_XA + 2, :])


def _xattn_prompt(x, l, norm_g, k16, v16, wq16, wo16, nb, tt):
    rows = tt * nb
    x_spec = pl.BlockSpec((rows, D), lambda i: (i, 0))
    return pl.pallas_call(
        functools.partial(_xattn_prompt_body, nb=nb),
        grid=(x.shape[0] // rows,),
        in_specs=[x_spec, _layer(k16.shape[1:], l), _layer(v16.shape[1:], l),
                  _layer(norm_g.shape[1:], l), _layer(wq16.shape[1:], l), _layer(wo16.shape[1:], l)],
        out_specs=x_spec,
        out_shape=jax.ShapeDtypeStruct(x.shape, F32),
        scratch_shapes=[pltpu.VMEM((D // LANES, rows, LANES), F32)] * 2,
        compiler_params=_params(("arbitrary",)),
        name="xattn_prompt",
    )(x, k16, v16, norm_g, wq16, wo16)


def _xattn_sample_body(x_ref, k_ref, v_ref, g_ref, wq_ref, wo_ref, o_ref, *, bb, tq):
    x = x_ref[...]
    u = _rms(x, g_ref[G_XA:G_XA + 1, :]).astype(BF16)
    q = _dot(u, wq_ref[...]) * (HEAD_DIM ** -0.5)
    n_q, n_k = HEADS * tq, N_MEM * HEADS
    row_head = lax.broadcasted_iota(jnp.int32, (n_q, n_k), 0) // tq
    col_head = lax.broadcasted_iota(jnp.int32, (n_q, n_k), 1) % HEADS
    own_head = row_head == col_head
    outs = []
    for b in range(bb):
        qb = q[b * tq:(b + 1) * tq]
        q2 = jnp.concatenate([qb[:, h * HEAD_DIM:(h + 1) * HEAD_DIM] for h in range(HEADS)], axis=0)
        k2 = k_ref[b].reshape(n_k, HEAD_DIM)
        v2 = v_ref[b].reshape(n_k, HEAD_DIM)
        s = lax.dot_general(q2, k2, (((1,), (1,)), ((), ())), preferred_element_type=F32)
        p = _softmax_rows(jnp.where(own_head, s, jnp.finfo(F32).min))
        o2 = _dot(p, v2)
        outs.append(jnp.concatenate([o2[h * tq:(h + 1) * tq] for h in range(HEADS)], axis=-1))
    o = jnp.concatenate(outs, axis=0)
    y = _dot(o.astype(BF16), wo_ref[...])
    o_ref[...] = x + _rms(y, g_ref[G_XA + 1:G_XA + 2, :])


def _xattn_sample(x, l, norm_g, cache_k, cache_v, wq16, wo16, tq, bb):
    n_batch = cache_k.shape[1]
    x_spec = pl.BlockSpec((bb * tq, D), lambda i: (i, 0))
    kv_spec = pl.BlockSpec((None, bb, N_MEM, HEADS, HEAD_DIM), lambda i: (l, i, 0, 0, 0))
    return pl.pallas_call(
        functools.partial(_xattn_sample_body, bb=bb, tq=tq),
        grid=(n_batch // bb,),
        in_specs=[x_spec, kv_spec, kv_spec, _layer(norm_g.shape[1:], l), _layer(wq16.shape[1:], l),
                  _layer(wo16.shape[1:], l)],
        out_specs=x_spec,
        out_shape=jax.ShapeDtypeStruct(x.shape, F32),
        compiler_params=_params(("arbitrary",)),
        name="xattn_sample",
    )(x, cache_k, cache_v, norm_g, wq16, wo16)


FIR_STEPS = SUBLANES


def _causal_conv(buf_ref, w_ref, b_ref, out_ref, n_taps, tt, nb):
    groups = nb // SUBLANES
    n_in = FIR_STEPS + n_taps - 1
    for c in range(D // LANES):
        cs = slice(c * LANES, (c + 1) * LANES)
        wk = [jnp.broadcast_to(w_ref[k:k + 1, cs], (SUBLANES, LANES)) for k in range(n_taps)]
        bias = jnp.broadcast_to(b_ref[0:1, cs], (SUBLANES, LANES))

        def body(i, carry):
            base = (i // groups) * (FIR_STEPS * nb) + (i % groups) * SUBLANES
            base = pl.multiple_of(base, SUBLANES)
            acc = [bias] * FIR_STEPS
            for j in range(n_in):
                xj = buf_ref[pl.ds(base + j * nb, SUBLANES), cs]
                for o in range(FIR_STEPS):
                    if 0 <= j - o < n_taps:
                        acc[o] = acc[o] + wk[j - o] * xj
            for o in range(FIR_STEPS):
                out_ref[pl.ds(base + o * nb, SUBLANES), cs] = acc[o]
            return carry

        lax.fori_loop(0, (tt // FIR_STEPS) * groups, body, 0)


def _mixer_body(*refs, tt, nb, has_state):
    if has_state:
        x_ref, ha_ref, hb_ref, h0_ref = refs[:4]
        refs = refs[4:]
    else:
        x_ref = refs[0]
        refs = refs[1:]
    (g_ref, win_ref, bin_ref, caw_ref, cab_ref, lng_ref, lnb_ref, wa_ref, cbw_ref, cbb_ref,
     lruw_ref, lba_ref, lbx_ref, lam_ref, wb_ref, wout_ref,
     o_ref, na_ref, nbuf_ref, nh_ref,
     abuf, bbuf, hbuf, conv_s, a_s, u_s) = refs
    rows = tt * nb
    ha_rows = (CONV_A - 1) * nb
    hb_rows = (CONV_B - 1) * nb
    ti = pl.program_id(1)

    @pl.when(ti == 0)
    def _():
        if has_state:
            abuf[0:ha_rows, :] = jnp.swapaxes(ha_ref[...], 0, 1).reshape(ha_rows, D)
            bbuf[0:hb_rows, :] = jnp.swapaxes(hb_ref[...], 0, 1).reshape(hb_rows, D)
            hbuf[...] = h0_ref[...]
        else:
            abuf[0:ha_rows, :] = jnp.zeros((ha_rows, D), F32)
            bbuf[0:hb_rows, :] = jnp.zeros((hb_rows, D), F32)
            hbuf[...] = jnp.zeros((nb, D), F32)

    if has_state:
        x = _to_time_major(x_ref[...].reshape(rows, D), nb, tt)
    else:
        x = x_ref[...]
    u = _rms(x, g_ref[G_MIX:G_MIX + 1, :]).astype(BF16)

    def proj(s):
        return _dot_cols(u, win_ref, s * D, D) + bin_ref[:, s * D:(s + 1) * D]

    abuf[ha_rows:ha_rows + rows, :] = proj(0) * _sigmoid(proj(1))
    _causal_conv(abuf, caw_ref, cab_ref, conv_s, CONV_A, tt, nb)
    abuf[0:ha_rows, :] = abuf[rows:rows + ha_rows, :]
    ac = conv_s[...]
    mu = jnp.mean(ac, axis=-1, keepdims=True)
    acc = ac - mu
    var = jnp.mean(acc * acc, axis=-1, keepdims=True)
    ln = acc * lax.rsqrt(var + EPS) * lng_ref[...] + lnb_ref[...]
    y_a = _dot_cols((ln * _sigmoid(ln)).astype(BF16), wa_ref, 0, D)

    bbuf[hb_rows:hb_rows + rows, :] = proj(2)
    _causal_conv(bbuf, cbw_ref, cbb_ref, conv_s, CONV_B, tt, nb)
    bbuf[0:hb_rows, :] = bbuf[rows:rows + hb_rows, :]
    xb = conv_s[...]
    xb16 = xb.astype(BF16)
    ri = [_dot(xb16[:, n * LRU_BLOCK:(n + 1) * LRU_BLOCK], lruw_ref[n]) for n in range(LRU_BLOCKS)]
    r_pre = jnp.concatenate([p[:, :LRU_BLOCK] for p in ri], axis=-1)
    i_pre = jnp.concatenate([p[:, LRU_BLOCK:] for p in ri], axis=-1)
    r = _sigmoid(r_pre + lba_ref[...])
    gi = _sigmoid(i_pre + lbx_ref[...])
    nlam = -lam_ref[...]
    softplus = jnp.maximum(nlam, 0.0) + jnp.log1p(jnp.exp(-jnp.abs(nlam)))
    log_a = (-LRU_C) * r * softplus
    a = jnp.exp(log_a)
    one_m_a2 = -jnp.tanh(log_a) * (a * a + 1.0)
    a_s[...] = a
    u_s[...] = jnp.sqrt(one_m_a2) * (gi * xb)

    def scan_step(t, h):
        r0 = pl.multiple_of(t * nb, nb)
        h = a_s[pl.ds(r0, nb), :] * h + u_s[pl.ds(r0, nb), :]
        u_s[pl.ds(r0, nb), :] = h
        return h

    h_last = lax.fori_loop(0, tt, scan_step, hbuf[...])
    hbuf[...] = h_last
    hs = u_s[...]
    bg = proj(3)
    gelu = 0.5 * bg * (1.0 + jnp.tanh(math.sqrt(2.0 / math.pi) * (bg + 0.044715 * (bg * bg * bg))))
    y_b = _dot_cols((hs * gelu).astype(BF16), wb_ref, 0, D)

    merged = _sigmoid(proj(4)) * y_a + _sigmoid(proj(5)) * y_b
    m = _dot_cols(merged.astype(BF16), wout_ref, 0, D)
    y = x + _rms(m, g_ref[G_MIX + 1:G_MIX + 2, :])
    if has_state:
        o_ref[...] = _to_batch_major(y, nb, tt).reshape(nb, tt, D)
    else:
        o_ref[...] = y

    @pl.when(ti == pl.num_programs(1) - 1)
    def _():
        na_ref[...] = jnp.swapaxes(abuf[0:ha_rows, :].reshape(CONV_A - 1, nb, D), 0, 1)
        nbuf_ref[...] = jnp.swapaxes(bbuf[0:hb_rows, :].reshape(CONV_B - 1, nb, D), 0, 1)
        nh_ref[...] = h_last


def _mixer(x, state, l, w, tt, nb):
    has_state = state is not None
    rows = tt * nb
    if has_state:
        n_b = x.shape[0]
        grid = (n_b // nb, 1)
        x_spec = pl.BlockSpec((nb, tt, D), lambda j, i: (j, 0, 0))
    else:
        n_b = nb
        grid = (1, x.shape[0] // rows)
        x_spec = pl.BlockSpec((rows, D), lambda j, i: (i, 0))

    def state_spec(n):
        return pl.BlockSpec((nb, n, D), lambda j, i: (j, 0, 0))

    h_spec = pl.BlockSpec((nb, D), lambda j, i: (j, 0))
    weights = [w["norm_g"], w["w_in"], w["b_in"], w["conv_a_w"], w["conv_a_b"], w["conv_ln_g"],
               w["conv_ln_b"], w["w_a_out"], w["conv_b_w"], w["conv_b_b"], w["lru_w"], w["lru_b_a"],
               w["lru_b_x"], w["lru_lambda"], w["w_b_out"], w["w_out"]]
    state_in = list(state) if has_state else []
    state_specs = [state_spec(CONV_A - 1), state_spec(CONV_B - 1), h_spec]
    state_in_specs = [pl.BlockSpec((None, nb, CONV_A - 1, D), lambda j, i: (l, j, 0, 0)),
                      pl.BlockSpec((None, nb, CONV_B - 1, D), lambda j, i: (l, j, 0, 0)),
                      pl.BlockSpec((None, nb, D), lambda j, i: (l, j, 0))]
    return pl.pallas_call(
        functools.partial(_mixer_body, tt=tt, nb=nb, has_state=has_state),
        grid=grid,
        in_specs=[x_spec] + (state_in_specs if has_state else [])
                 + [_layer(a.shape[1:], l) for a in weights],
        out_specs=[x_spec] + state_specs,
        out_shape=[jax.ShapeDtypeStruct(x.shape, F32),
                   jax.ShapeDtypeStruct((n_b, CONV_A - 1, D), F32),
                   jax.ShapeDtypeStruct((n_b, CONV_B - 1, D), F32),
                   jax.ShapeDtypeStruct((n_b, D), F32)],
        scratch_shapes=[pltpu.VMEM(((CONV_A - 1) * nb + rows, D), F32),
                        pltpu.VMEM(((CONV_B - 1) * nb + rows, D), F32),
                        pltpu.VMEM((nb, D), F32),
                        pltpu.VMEM((rows, D), F32),
                        pltpu.VMEM((rows, D), F32),
                        pltpu.VMEM((rows, D), F32)],
        compiler_params=_params(("arbitrary", "arbitrary")),
        name="mixer",
    )(x, *state_in, *weights)


def kernel(x_prompt, x_sample, mem_prompt, cache_mem_k, cache_mem_v, state_conv_a, state_conv_b, state_lru_h, w_in, b_in, conv_a_w, conv_a_b, conv_ln_g, conv_ln_b, w_a_out, conv_b_w, conv_b_b, lru_w_a, lru_b_a, lru_w_x, lru_b_x, lru_lambda, w_b_out, w_out, xa_w_q, xa_w_k, xa_w_v, xa_w_o, ffn1_w_in, ffn1_w_out, ffn2_w_in, ffn2_w_out, norm_g):
    n_b, n_t, _ = x_prompt.shape
    s_b, s_t, _ = x_sample.shape

    def vec(a):
        return a.reshape(DEPTH, 1, a.shape[-1])

    mixer_w = dict(
        norm_g=norm_g, w_in=w_in.astype(BF16), b_in=vec(b_in), conv_a_w=conv_a_w,
        conv_a_b=vec(conv_a_b), conv_ln_g=vec(conv_ln_g), conv_ln_b=vec(conv_ln_b),
        w_a_out=w_a_out.astype(BF16), conv_b_w=conv_b_w, conv_b_b=vec(conv_b_b),
        lru_w=jnp.concatenate([lru_w_a, lru_w_x], axis=-1).astype(BF16),
        lru_b_a=vec(lru_b_a), lru_b_x=vec(lru_b_x), lru_lambda=vec(lru_lambda),
        w_b_out=w_b_out.astype(BF16), w_out=w_out.astype(BF16))
    ffn_w = {G_FFN1: (ffn1_w_in.astype(BF16), ffn1_w_out.astype(BF16)),
             G_FFN2: (ffn2_w_in.astype(BF16), ffn2_w_out.astype(BF16))}
    wq16 = xa_w_q.reshape(DEPTH, D, D).astype(BF16)
    wo16 = xa_w_o.reshape(DEPTH, D, D).astype(BF16)

    mem_k, mem_v, k16, v16 = _memkv(mem_prompt.reshape(n_b * N_MEM, D), norm_g,
                                    xa_w_k.reshape(DEPTH, D, D).astype(BF16),
                                    xa_w_v.reshape(DEPTH, D, D).astype(BF16))

    tm = 512
    p_a, p_b, p_h = [], [], []
    x = x_prompt
    for l in range(DEPTH):
        x = _ffn(x, l, G_FFN1, norm_g, *ffn_w[G_FFN1], tm=tm, relayout="in" if l == 0 else None)
        x, na, nbuf, nh = _mixer(x, None, l, mixer_w, tt=64, nb=n_b)
        p_a.append(na)
        p_b.append(nbuf)
        p_h.append(nh)
        x = _xattn_prompt(x, l, norm_g, k16, v16, wq16, wo16, nb=n_b, tt=128)
        x = _ffn(x, l, G_FFN2, norm_g, *ffn_w[G_FFN2], tm=tm,
                 relayout="out" if l == DEPTH - 1 else None, n_batch=n_b)
    y_prompt = x

    ts = 256
    s_a, s_bb, s_h = [], [], []
    x = x_sample.reshape(s_b * s_t, D)
    for l in range(DEPTH):
        x = _ffn(x, l, G_FFN1, norm_g, *ffn_w[G_FFN1], tm=ts)
        x, na, nbuf, nh = _mixer(x.reshape(s_b, s_t, D),
                                 (state_conv_a, state_conv_b, state_lru_h),
                                 l, mixer_w, tt=s_t, nb=32)
        s_a.append(na)
        s_bb.append(nbuf)
        s_h.append(nh)
        x = _xattn_sample(x.reshape(s_b * s_t, D), l, norm_g, cache_mem_k, cache_mem_v, wq16, wo16,
                          tq=s_t, bb=8)
        x = _ffn(x, l, G_FFN2, norm_g, *ffn_w[G_FFN2], tm=ts)
    y_sample = x.reshape(s_b, s_t, D)

    kv_shape = (DEPTH, n_b, N_MEM, HEADS, HEAD_DIM)
    return (y_prompt, y_sample, mem_k.reshape(kv_shape), mem_v.reshape(kv_shape),
            jnp.stack(p_a), jnp.stack(p_b), jnp.stack(p_h),
            jnp.stack(s_a), jnp.stack(s_bb), jnp.stack(s_h))
```

```python
import functools
import math

import jax
import jax.numpy as jnp
from jax import lax
from jax.experimental import pallas as pl
from jax.experimental.pallas import tpu as pltpu

D = 1024
DEPTH = 2
N_MEM = 256
HEADS = 4
HEAD_DIM = D // HEADS
CONV_A = 31
CONV_B = 4
LRU_BLOCKS = 8
LRU_BLOCK = D // LRU_BLOCKS
LRU_C = 8.0
D_FF = 2816
EPS = 1e-6
G_FFN1, G_MIX, G_XA, G_FFN2 = 0, 2, 4, 7

LANES = 128
SUBLANES = 8
MXU_N = 256
FF_CHUNK = MXU_N
N_FF_CHUNKS = D_FF // FF_CHUNK
VMEM_LIMIT = 56 * 1024 * 1024

F32 = jnp.float32
BF16 = jnp.bfloat16


def _dot(a, b):
    return jnp.dot(a, b, preferred_element_type=F32)


def _dot_cols(a, w_ref, col0, n_cols):
    return jnp.concatenate(
        [_dot(a, w_ref[:, col0 + j:col0 + j + MXU_N]) for j in range(0, n_cols, MXU_N)], axis=-1)


def _rms(x, g):
    ms = jnp.mean(x * x, axis=-1, keepdims=True)
    return x * lax.rsqrt(ms + EPS) * g


def _sigmoid(x):
    return 0.5 + 0.5 * jnp.tanh(0.5 * x)


def _layer(shape, l):
    nd = len(shape)
    return pl.BlockSpec((None,) + tuple(shape), lambda *_: (l,) + (0,) * nd,
                        pipeline_mode=pl.Buffered(1))


def _params(sem):
    return pltpu.CompilerParams(dimension_semantics=sem, vmem_limit_bytes=VMEM_LIMIT)


def _to_time_major(v, nb, tt):
    return jnp.swapaxes(v.reshape(nb, tt, D), 0, 1).reshape(tt * nb, D)


def _to_batch_major(v, nb, tt):
    return jnp.swapaxes(v.reshape(tt, nb, D), 0, 1).reshape(nb * tt, D)


def _ffn_body(x_ref, g_ref, win_ref, wout_ref, o_ref, *, g_row, relayout):
    if relayout == "in":
        nb, tq, _ = x_ref.shape
        x = x_ref[...].reshape(nb * tq, D)
    else:
        x = x_ref[...]
    xn = _rms(x, g_ref[g_row:g_row + 1, :]).astype(BF16)
    acc = None
    for j in range(N_FF_CHUNKS):
        lo = j * FF_CHUNK
        gate = _dot(xn, win_ref[:, lo:lo + FF_CHUNK])
        up = _dot(xn, win_ref[:, D_FF + lo:D_FF + lo + FF_CHUNK])
        h = (gate * _sigmoid(gate) * up).astype(BF16)
        d = _dot(h, wout_ref[lo:lo + FF_CHUNK, :])
        acc = d if acc is None else acc + d
    y = x + 0.5 * _rms(acc, g_ref[g_row + 1:g_row + 2, :])
    if relayout == "in":
        o_ref[...] = _to_time_major(y, nb, tq)
    elif relayout == "out":
        nb, tq, _ = o_ref.shape
        o_ref[...] = _to_batch_major(y, nb, tq).reshape(nb, tq, D)
    else:
        o_ref[...] = y


def _ffn(x, l, g_row, norm_g, w_in16, w_out16, tm, relayout=None, n_batch=None):
    if relayout == "in":
        n_b, n_t, _ = x.shape
        tq = tm // n_b
        grid = (n_t // tq,)
        x_spec = pl.BlockSpec((n_b, tq, D), lambda i: (0, i, 0))
        o_spec = pl.BlockSpec((tm, D), lambda i: (i, 0))
        out_shape = (n_t * n_b, D)
    elif relayout == "out":
        rows = x.shape[0]
        tq = tm // n_batch
        grid = (rows // tm,)
        x_spec = pl.BlockSpec((tm, D), lambda i: (i, 0))
        o_spec = pl.BlockSpec((n_batch, tq, D), lambda i: (0, i, 0))
        out_shape = (n_batch, rows // n_batch, D)
    else:
        rows = x.shape[0]
        grid = (rows // tm,)
        x_spec = o_spec = pl.BlockSpec((tm, D), lambda i: (i, 0))
        out_shape = (rows, D)
    return pl.pallas_call(
        functools.partial(_ffn_body, g_row=g_row, relayout=relayout),
        grid=grid,
        in_specs=[x_spec, _layer(norm_g.shape[1:], l), _layer(w_in16.shape[1:], l),
                  _layer(w_out16.shape[1:], l)],
        out_specs=o_spec,
        out_shape=jax.ShapeDtypeStruct(out_shape, F32),
        compiler_params=_params(("arbitrary",)),
        name="ffn",
    )(x, norm_g, w_in16, w_out16)


def _memkv_body(m_ref, g_ref, wk_ref, wv_ref, k_ref, v_ref, k16_ref, v16_ref):
    mn = _rms(m_ref[...], g_ref[6:7, :]).astype(BF16)
    k = _dot(mn, wk_ref[...])
    v = _dot(mn, wv_ref[...])
    k_ref[...] = k.reshape(k_ref.shape)
    v_ref[...] = v.reshape(v_ref.shape)
    k16_ref[...] = k.astype(BF16)
    v16_ref[...] = v.astype(BF16)


def _memkv(mem2d, norm_g, wk16, wv16, tm=512):
    rows = mem2d.shape[0]
    w_spec = pl.BlockSpec((None, D, D), lambda l, i: (l, 0, 0))
    o_spec = pl.BlockSpec((None, tm, D), lambda l, i: (l, i, 0))
    o5_spec = pl.BlockSpec((None, tm // N_MEM, N_MEM, HEADS, HEAD_DIM), lambda l, i: (l, i, 0, 0, 0))
    return pl.pallas_call(
        _memkv_body,
        grid=(DEPTH, rows // tm),
        in_specs=[pl.BlockSpec((tm, D), lambda l, i: (i, 0)),
                  pl.BlockSpec((None,) + norm_g.shape[1:], lambda l, i: (l, 0, 0)),
                  w_spec, w_spec],
        out_specs=[o5_spec] * 2 + [o_spec] * 2,
        out_shape=[jax.ShapeDtypeStruct((DEPTH, rows // N_MEM, N_MEM, HEADS, HEAD_DIM), F32)] * 2
                  + [jax.ShapeDtypeStruct((DEPTH, rows, D), BF16)] * 2,
        compiler_params=_params(("arbitrary", "arbitrary")),
        name="memkv",
    )(mem2d, norm_g, wk16, wv16)


def _softmax_rows(s):
    e = jnp.exp(s - jnp.max(s, axis=-1, keepdims=True))
    return e * (1.0 / jnp.sum(e, axis=-1, keepdims=True))


def _xattn_prompt_body(x_ref, k_ref, v_ref, g_ref, wq_ref, wo_ref, o_ref, q_s, o_s, *, nb):
    rows = x_ref.shape[0]
    tt = rows // nb
    n_lb = D // LANES
    lb_per_head = HEAD_DIM // LANES
    x = x_ref[...]
    u = _rms(x, g_ref[G_XA:G_XA + 1, :]).astype(BF16)
    q = _dot(u, wq_ref[...]) * (HEAD_DIM ** -0.5)
    for c in range(n_lb):
        q_s[c] = q[:, c * LANES:(c + 1) * LANES]
    scores = []
    for b in range(nb):
        for h in range(HEADS):
            qbh = jnp.concatenate(
                [q_s[h * lb_per_head + i, pl.ds(b, tt, stride=nb), :] for i in range(lb_per_head)],
                axis=-1).astype(BF16)
            scores.append(lax.dot_general(
                qbh, k_ref[b * N_MEM:(b + 1) * N_MEM, h * HEAD_DIM:(h + 1) * HEAD_DIM],
                (((1,), (1,)), ((), ())), preferred_element_type=F32))
    p = _softmax_rows(jnp.stack(scores)).astype(BF16)
    for b in range(nb):
        for h in range(HEADS):
            obh = _dot(p[b * HEADS + h],
                       v_ref[b * N_MEM:(b + 1) * N_MEM, h * HEAD_DIM:(h + 1) * HEAD_DIM])
            for i in range(lb_per_head):
                o_s[h * lb_per_head + i, pl.ds(b, tt, stride=nb), :] = obh[:, i * LANES:(i + 1) * LANES]
    o = jnp.concatenate([o_s[c] for c in range(n_lb)], axis=-1)
    y = _dot(o.astype(BF16), wo_ref[...])
    o_ref[...] = x + _rms(y, g_ref[G_XA + 1:G_XA + 2, :])


def _xattn_prompt(x, l, norm_g, k16, v16, wq16, wo16, nb, tt):
    rows = tt * nb
    x_spec = pl.BlockSpec((rows, D), lambda i: (i, 0))
    return pl.pallas_call(
        functools.partial(_xattn_prompt_body, nb=nb),
        grid=(x.shape[0] // rows,),
        in_specs=[x_spec, _layer(k16.shape[1:], l), _layer(v16.shape[1:], l),
                  _layer(norm_g.shape[1:], l), _layer(wq16.shape[1:], l), _layer(wo16.shape[1:], l)],
        out_specs=x_spec,
        out_shape=jax.ShapeDtypeStruct(x.shape, F32),
        scratch_shapes=[pltpu.VMEM((D // LANES, rows, LANES), F32)] * 2,
        compiler_params=_params(("arbitrary",)),
        name="xattn_prompt",
    )(x, k16, v16, norm_g, wq16, wo16)


def _xattn_sample_body(x_ref, k_ref, v_ref, g_ref, wq_ref, wo_ref, o_ref, *, bb, tq):
    x = x_ref[...]
    u = _rms(x, g_ref[G_XA:G_XA + 1, :]).astype(BF16)
    q = _dot(u, wq_ref[...]) * (HEAD_DIM ** -0.5)
    n_q, n_k = HEADS * tq, N_MEM * HEADS
    row_head = lax.broadcasted_iota(jnp.int32, (n_q, n_k), 0) // tq
    col_head = lax.broadcasted_iota(jnp.int32, (n_q, n_k), 1) % HEADS
    own_head = row_head == col_head
    outs = []
    for b in range(bb):
        qb = q[b * tq:(b + 1) * tq]
        q2 = jnp.concatenate([qb[:, h * HEAD_DIM:(h + 1) * HEAD_DIM] for h in range(HEADS)], axis=0)
        k2 = k_ref[b].reshape(n_k, HEAD_DIM)
        v2 = v_ref[b].reshape(n_k, HEAD_DIM)
        s = lax.dot_general(q2, k2, (((1,), (1,)), ((), ())), preferred_element_type=F32)
        p = _softmax_rows(jnp.where(own_head, s, jnp.finfo(F32).min))
        o2 = _dot(p, v2)
        outs.append(jnp.concatenate([o2[h * tq:(h + 1) * tq] for h in range(HEADS)], axis=-1))
    o = jnp.concatenate(outs, axis=0)
    y = _dot(o.astype(BF16), wo_ref[...])
    o_ref[...] = x + _rms(y, g_ref[G_XA + 1:G_XA + 2, :])


def _xattn_sample(x, l, norm_g, cache_k, cache_v, wq16, wo16, tq, bb):
    n_batch = cache_k.shape[1]
    x_spec = pl.BlockSpec((bb * tq, D), lambda i: (i, 0))
    kv_spec = pl.BlockSpec((None, bb, N_MEM, HEADS, HEAD_DIM), lambda i: (l, i, 0, 0, 0))
    return pl.pallas_call(
        functools.partial(_xattn_sample_body, bb=bb, tq=tq),
        grid=(n_batch // bb,),
        in_specs=[x_spec, kv_spec, kv_spec, _layer(norm_g.shape[1:], l), _layer(wq16.shape[1:], l),
                  _layer(wo16.shape[1:], l)],
        out_specs=x_spec,
        out_shape=jax.ShapeDtypeStruct(x.shape, F32),
        compiler_params=_params(("arbitrary",)),
        name="xattn_sample",
    )(x, cache_k, cache_v, norm_g, wq16, wo16)


FIR_STEPS = SUBLANES


def _causal_conv(buf_ref, w_ref, b_ref, out_ref, n_taps, tt, nb, side_work=None, side_per_step=0,
                 unrolled=False):
    groups = nb // SUBLANES
    assert groups & (groups - 1) == 0
    n_in = FIR_STEPS + n_taps - 1
    n_steps = (tt // FIR_STEPS) * groups
    lane_blocks = [slice(c * LANES, (c + 1) * LANES) for c in range(D // LANES)]

    def taps(cs):
        return [jnp.broadcast_to(w_ref[k:k + 1, cs], (SUBLANES, LANES)) for k in range(n_taps)]

    def fir_step(s, cs, wk, bias):
        if isinstance(s, int):
            base = (s // groups) * (FIR_STEPS * nb) + (s % groups) * SUBLANES
        else:
            t_chunk = lax.shift_right_logical(s, groups.bit_length() - 1)
            base = t_chunk * (FIR_STEPS * nb) + (s & (groups - 1)) * SUBLANES
            base = pl.multiple_of(base, SUBLANES)
        acc = [bias] * FIR_STEPS
        for j in range(n_in):
            xj = buf_ref[pl.ds(base + j * nb, SUBLANES), cs]
            for o in range(FIR_STEPS):
                if 0 <= j - o < n_taps:
                    acc[o] = acc[o] + wk[j - o] * xj
        for o in range(FIR_STEPS):
            out_ref[pl.ds(base + o * nb, SUBLANES), cs] = acc[o]

    def bias_of(cs):
        return jnp.broadcast_to(b_ref[0:1, cs], (SUBLANES, LANES))

    if unrolled:
        for cs in lane_blocks:
            wk, bias = taps(cs), bias_of(cs)
            for s in range(n_steps):
                fir_step(s, cs, wk, bias)
    elif side_work is None:
        for cs in lane_blocks:
            wk, bias = taps(cs), bias_of(cs)
            lax.fori_loop(0, n_steps, lambda s, carry: (fir_step(s, cs, wk, bias), carry)[1], 0)
    else:
        def body(s, carry):
            for cs in lane_blocks:
                fir_step(s, cs, taps(cs), bias_of(cs))
            for p in range(side_per_step):
                side_work(s * side_per_step + p)
            return carry

        lax.fori_loop(0, n_steps, body, 0)
    return n_steps * side_per_step


def _mixer_body(*refs, tt, nb, has_state):
    if has_state:
        x_ref, ha_ref, hb_ref, h0_ref = refs[:4]
        refs = refs[4:]
    else:
        x_ref = refs[0]
        refs = refs[1:]
    (g_ref, win_ref, bin_ref, caw_ref, cab_ref, lng_ref, lnb_ref, wa_ref, cbw_ref, cbb_ref,
     lruw_ref, lba_ref, lbx_ref, lam_ref, wb_ref, wout_ref,
     o_ref, na_ref, nbuf_ref, nh_ref,
     abuf, bbuf, hbuf, conv_s, a_s, u_s, xn_s, proj_s) = refs
    rows = tt * nb
    n_cc = D // MXU_N
    piece_rows = min(rows, 256)
    row_splits = rows // piece_rows
    ha_rows = (CONV_A - 1) * nb
    hb_rows = (CONV_B - 1) * nb
    ti = pl.program_id(1)

    @pl.when(ti == 0)
    def _():
        if has_state:
            abuf[0:ha_rows, :] = jnp.swapaxes(ha_ref[...], 0, 1).reshape(ha_rows, D)
            bbuf[0:hb_rows, :] = jnp.swapaxes(hb_ref[...], 0, 1).reshape(hb_rows, D)
            hbuf[...] = h0_ref[...]
        else:
            abuf[0:ha_rows, :] = jnp.zeros((ha_rows, D), F32)
            bbuf[0:hb_rows, :] = jnp.zeros((hb_rows, D), F32)
            hbuf[...] = jnp.zeros((nb, D), F32)

    if has_state:
        x = _to_time_major(x_ref[...].reshape(rows, D), nb, tt)
    else:
        x = x_ref[...]
    u = _rms(x, g_ref[G_MIX:G_MIX + 1, :]).astype(BF16)
    xn_s[...] = u

    def proj_now(s):
        cols = [_dot(u, win_ref[s * n_cc + j]) for j in range(n_cc)]
        return jnp.concatenate(cols, axis=-1) + bin_ref[:, s * D:(s + 1) * D]

    def proj_piece(p):
        if isinstance(p, int):
            q, r0 = p // row_splits, (p % row_splits) * piece_rows
        elif row_splits == 1:
            q, r0 = p, 0
        else:
            q = lax.shift_right_logical(p, row_splits.bit_length() - 1)
            r0 = pl.multiple_of((p & (row_splits - 1)) * piece_rows, piece_rows)
        proj_s[q, pl.ds(r0, piece_rows), :] = _dot(xn_s[pl.ds(r0, piece_rows), :],
                                                   win_ref[2 * n_cc + q])

    def proj(s):
        cols = [proj_s[(s - 2) * n_cc + j] for j in range(n_cc)]
        return jnp.concatenate(cols, axis=-1) + bin_ref[:, s * D:(s + 1) * D]

    abuf[ha_rows:ha_rows + rows, :] = proj_now(0) * _sigmoid(proj_now(1))
    n_steps = (tt // FIR_STEPS) * (nb // SUBLANES)
    n_pieces = _causal_conv(abuf, caw_ref, cab_ref, conv_s, CONV_A, tt, nb, side_work=proj_piece,
                            side_per_step=4 * n_cc * row_splits // n_steps)
    assert n_pieces == 4 * n_cc * row_splits
    abuf[0:ha_rows, :] = abuf[rows:rows + ha_rows, :]
    ac = conv_s[...]
    mu = jnp.mean(ac, axis=-1, keepdims=True)
    acc = ac - mu
    var = jnp.mean(acc * acc, axis=-1, keepdims=True)
    ln = acc * lax.rsqrt(var + EPS) * lng_ref[...] + lnb_ref[...]
    y_a = _dot_cols((ln * _sigmoid(ln)).astype(BF16), wa_ref, 0, D)

    bbuf[hb_rows:hb_rows + rows, :] = proj(2)
    _causal_conv(bbuf, cbw_ref, cbb_ref, conv_s, CONV_B, tt, nb, unrolled=True)
    bbuf[0:hb_rows, :] = bbuf[rows:rows + hb_rows, :]
    xb = conv_s[...]
    xb16 = xb.astype(BF16)
    ri = [_dot(xb16[:, n * LRU_BLOCK:(n + 1) * LRU_BLOCK], lruw_ref[n]) for n in range(LRU_BLOCKS)]
    r_pre = jnp.concatenate([p[:, :LRU_BLOCK] for p in ri], axis=-1)
    i_pre = jnp.concatenate([p[:, LRU_BLOCK:] for p in ri], axis=-1)
    r = _sigmoid(r_pre + lba_ref[...])
    gi = _sigmoid(i_pre + lbx_ref[...])
    nlam = -lam_ref[...]
    softplus = jnp.maximum(nlam, 0.0) + jnp.log1p(jnp.exp(-jnp.abs(nlam)))
    log_a = (-LRU_C) * r * softplus
    a = jnp.exp(log_a)
    one_m_a2 = -jnp.tanh(log_a) * (a * a + 1.0)
    a_s[...] = a
    u_s[...] = jnp.sqrt(one_m_a2) * (gi * xb)

    def scan_step(t, h):
        r0 = pl.multiple_of(t * nb, nb)
        h = a_s[pl.ds(r0, nb), :] * h + u_s[pl.ds(r0, nb), :]
        u_s[pl.ds(r0, nb), :] = h
        return h

    h_last = lax.fori_loop(0, tt, scan_step, hbuf[...])
    hbuf[...] = h_last
    hs = u_s[...]
    bg = proj(3)
    gelu = 0.5 * bg * (1.0 + jnp.tanh(math.sqrt(2.0 / math.pi) * (bg + 0.044715 * (bg * bg * bg))))
    y_b = _dot_cols((hs * gelu).astype(BF16), wb_ref, 0, D)

    merged = _sigmoid(proj(4)) * y_a + _sigmoid(proj(5)) * y_b
    m = _dot_cols(merged.astype(BF16), wout_ref, 0, D)
    y = x + _rms(m, g_ref[G_MIX + 1:G_MIX + 2, :])
    if has_state:
        o_ref[...] = _to_batch_major(y, nb, tt).reshape(nb, tt, D)
    else:
        o_ref[...] = y

    @pl.when(ti == pl.num_programs(1) - 1)
    def _():
        na_ref[...] = jnp.swapaxes(abuf[0:ha_rows, :].reshape(CONV_A - 1, nb, D), 0, 1)
        nbuf_ref[...] = jnp.swapaxes(bbuf[0:hb_rows, :].reshape(CONV_B - 1, nb, D), 0, 1)
        nh_ref[...] = h_last


def _mixer(x, state, l, w, tt, nb):
    has_state = state is not None
    rows = tt * nb
    if has_state:
        n_b = x.shape[0]
        grid = (n_b // nb, 1)
        x_spec = pl.BlockSpec((nb, tt, D), lambda j, i: (j, 0, 0))
    else:
        n_b = nb
        grid = (1, x.shape[0] // rows)
        x_spec = pl.BlockSpec((rows, D), lambda j, i: (i, 0))

    def state_spec(n):
        return pl.BlockSpec((nb, n, D), lambda j, i: (j, 0, 0), pipeline_mode=pl.Buffered(1))

    h_spec = pl.BlockSpec((nb, D), lambda j, i: (j, 0))
    weights = [w["norm_g"], w["w_in"], w["b_in"], w["conv_a_w"], w["conv_a_b"], w["conv_ln_g"],
               w["conv_ln_b"], w["w_a_out"], w["conv_b_w"], w["conv_b_b"], w["lru_w"], w["lru_b_a"],
               w["lru_b_x"], w["lru_lambda"], w["w_b_out"], w["w_out"]]
    state_in = list(state) if has_state else []
    state_specs = [state_spec(CONV_A - 1), state_spec(CONV_B - 1), h_spec]
    state_in_specs = [pl.BlockSpec((None, nb, CONV_A - 1, D), lambda j, i: (l, j, 0, 0),
                                   pipeline_mode=pl.Buffered(1)),
                      pl.BlockSpec((None, nb, CONV_B - 1, D), lambda j, i: (l, j, 0, 0)),
                      pl.BlockSpec((None, nb, D), lambda j, i: (l, j, 0))]
    return pl.pallas_call(
        functools.partial(_mixer_body, tt=tt, nb=nb, has_state=has_state),
        grid=grid,
        in_specs=[x_spec] + (state_in_specs if has_state else [])
                 + [_layer(a.shape[1:], l) for a in weights],
        out_specs=[x_spec] + state_specs,
        out_shape=[jax.ShapeDtypeStruct(x.shape, F32),
                   jax.ShapeDtypeStruct((n_b, CONV_A - 1, D), F32),
                   jax.ShapeDtypeStruct((n_b, CONV_B - 1, D), F32),
                   jax.ShapeDtypeStruct((n_b, D), F32)],
        scratch_shapes=[pltpu.VMEM(((CONV_A - 1) * nb + rows, D), F32),
                        pltpu.VMEM(((CONV_B - 1) * nb + rows, D), F32),
                        pltpu.VMEM((nb, D), F32),
                        pltpu.VMEM((rows, D), F32),
                        pltpu.VMEM((rows, D), F32),
                        pltpu.VMEM((rows, D), F32),
                        pltpu.VMEM((rows, D), BF16),
                        pltpu.VMEM((4 * D // MXU_N, rows, MXU_N), F32)],
        compiler_params=_params(("arbitrary", "arbitrary")),
        name="mixer",
    )(x, *state_in, *weights)


def kernel(x_prompt, x_sample, mem_prompt, cache_mem_k, cache_mem_v, state_conv_a, state_conv_b, state_lru_h, w_in, b_in, conv_a_w, conv_a_b, conv_ln_g, conv_ln_b, w_a_out, conv_b_w, conv_b_b, lru_w_a, lru_b_a, lru_w_x, lru_b_x, lru_lambda, w_b_out, w_out, xa_w_q, xa_w_k, xa_w_v, xa_w_o, ffn1_w_in, ffn1_w_out, ffn2_w_in, ffn2_w_out, norm_g):
    n_b, n_t, _ = x_prompt.shape
    s_b, s_t, _ = x_sample.shape

    def vec(a):
        return a.reshape(DEPTH, 1, a.shape[-1])

    mixer_w = dict(
        norm_g=norm_g,
        w_in=w_in.reshape(DEPTH, D, 6 * D // MXU_N, MXU_N).transpose(0, 2, 1, 3).astype(BF16),
        b_in=vec(b_in), conv_a_w=conv_a_w,
        conv_a_b=vec(conv_a_b), conv_ln_g=vec(conv_ln_g), conv_ln_b=vec(conv_ln_b),
        w_a_out=w_a_out.astype(BF16), conv_b_w=conv_b_w, conv_b_b=vec(conv_b_b),
        lru_w=jnp.concatenate([lru_w_a, lru_w_x], axis=-1).astype(BF16),
        lru_b_a=vec(lru_b_a), lru_b_x=vec(lru_b_x), lru_lambda=vec(lru_lambda),
        w_b_out=w_b_out.astype(BF16), w_out=w_out.astype(BF16))
    ffn_w = {G_FFN1: (ffn1_w_in.astype(BF16), ffn1_w_out.astype(BF16)),
             G_FFN2: (ffn2_w_in.astype(BF16), ffn2_w_out.astype(BF16))}
    wq16 = xa_w_q.reshape(DEPTH, D, D).astype(BF16)
    wo16 = xa_w_o.reshape(DEPTH, D, D).astype(BF16)

    mem_k, mem_v, k16, v16 = _memkv(mem_prompt.reshape(n_b * N_MEM, D), norm_g,
                                    xa_w_k.reshape(DEPTH, D, D).astype(BF16),
                                    xa_w_v.reshape(DEPTH, D, D).astype(BF16))

    tm = 512
    p_a, p_b, p_h = [], [], []
    x = x_prompt
    for l in range(DEPTH):
        x = _ffn(x, l, G_FFN1, norm_g, *ffn_w[G_FFN1], tm=tm, relayout="in" if l == 0 else None)
        x, na, nbuf, nh = _mixer(x, None, l, mixer_w, tt=64, nb=n_b)
        p_a.append(na)
        p_b.append(nbuf)
        p_h.append(nh)
        x = _xattn_prompt(x, l, norm_g, k16, v16, wq16, wo16, nb=n_b, tt=128)
        x = _ffn(x, l, G_FFN2, norm_g, *ffn_w[G_FFN2], tm=tm,
                 relayout="out" if l == DEPTH - 1 else None, n_batch=n_b)
    y_prompt = x

    ts = 512
    s_a, s_bb, s_h = [], [], []
    x = x_sample.reshape(s_b * s_t, D)
    for l in range(DEPTH):
        x = _ffn(x, l, G_FFN1, norm_g, *ffn_w[G_FFN1], tm=ts)
        x, na, nbuf, nh = _mixer(x.reshape(s_b, s_t, D),
                                 (state_conv_a, state_conv_b, state_lru_h),
                                 l, mixer_w, tt=s_t, nb=32)
        s_a.append(na)
        s_bb.append(nbuf)
        s_h.append(nh)
        x = _xattn_sample(x.reshape(s_b * s_t, D), l, norm_g, cache_mem_k, cache_mem_v, wq16, wo16,
                          tq=s_t, bb=8)
        x = _ffn(x, l, G_FFN2, norm_g, *ffn_w[G_FFN2], tm=ts)
    y_sample = x.reshape(s_b, s_t, D)

    return (y_prompt, y_sample, mem_k, mem_v,
            jnp.stack(p_a), jnp.stack(p_b), jnp.stack(p_h),
            jnp.stack(s_a), jnp.stack(s_bb), jnp.stack(s_h))
```

```python
import functools
import math

import jax
import jax.numpy as jnp
from jax import lax
from jax.experimental import pallas as pl
from jax.experimental.pallas import tpu as pltpu

D = 1024
DEPTH = 2
N_MEM = 256
HEADS = 4
HEAD_DIM = D // HEADS
CONV_A = 31
CONV_B = 4
LRU_BLOCKS = 8
LRU_BLOCK = D // LRU_BLOCKS
LRU_C = 8.0
D_FF = 2816
EPS = 1e-6
G_FFN1, G_MIX, G_XA, G_FFN2 = 0, 2, 4, 7
V_CONV_A_B, V_LN_G, V_LN_B, V_CONV_B_B, V_LRU_B_A, V_LRU_B_X, V_LAMBDA, V_BIN = 9, 10, 11, 12, 13, 14, 15, 16
N_VEC_ROWS = 24

LANES = 128
SUBLANES = 8
MXU_N = 256
FF_CHUNK = MXU_N
N_FF_CHUNKS = D_FF // FF_CHUNK
VMEM_LIMIT = 56 * 1024 * 1024

F32 = jnp.float32
BF16 = jnp.bfloat16


def _dot(a, b):
    return jnp.dot(a, b, preferred_element_type=F32)


def _dot_cols(a, w_ref, col0, n_cols):
    return jnp.concatenate(
        [_dot(a, w_ref[:, col0 + j:col0 + j + MXU_N]) for j in range(0, n_cols, MXU_N)], axis=-1)


def _rms(x, g):
    ms = jnp.mean(x * x, axis=-1, keepdims=True)
    return x * lax.rsqrt(ms + EPS) * g


def _sigmoid(x):
    return 0.5 + 0.5 * jnp.tanh(0.5 * x)


def _layer(shape, l):
    nd = len(shape)
    return pl.BlockSpec((None,) + tuple(shape), lambda *_: (l,) + (0,) * nd,
                        pipeline_mode=pl.Buffered(1))


def _params(sem):
    return pltpu.CompilerParams(dimension_semantics=sem, vmem_limit_bytes=VMEM_LIMIT)


def _to_time_major(v, nb, tt):
    return jnp.swapaxes(v.reshape(nb, tt, D), 0, 1).reshape(tt * nb, D)


def _to_batch_major(v, nb, tt):
    return jnp.swapaxes(v.reshape(tt, nb, D), 0, 1).reshape(nb * tt, D)


def _ffn_body(x_ref, g_ref, win_ref, wout_ref, o_ref, *, g_row, relayout):
    if relayout == "in":
        nb, tq, _ = x_ref.shape
        x = x_ref[...].reshape(nb * tq, D)
    else:
        x = x_ref[...]
    xn = _rms(x, g_ref[g_row:g_row + 1, :]).astype(BF16)
    acc = None
    for j in range(N_FF_CHUNKS):
        lo = j * FF_CHUNK
        gate = _dot(xn, win_ref[:, lo:lo + FF_CHUNK])
        up = _dot(xn, win_ref[:, D_FF + lo:D_FF + lo + FF_CHUNK])
        h = (gate * _sigmoid(gate) * up).astype(BF16)
        d = _dot(h, wout_ref[lo:lo + FF_CHUNK, :])
        acc = d if acc is None else acc + d
    y = x + 0.5 * _rms(acc, g_ref[g_row + 1:g_row + 2, :])
    if relayout == "in":
        o_ref[...] = _to_time_major(y, nb, tq)
    elif relayout == "out":
        nb, tq, _ = o_ref.shape
        o_ref[...] = _to_batch_major(y, nb, tq).reshape(nb, tq, D)
    else:
        o_ref[...] = y


def _ffn(x, l, g_row, norm_g, w_in16, w_out16, tm, relayout=None, n_batch=None):
    if relayout == "in":
        n_b, n_t, _ = x.shape
        tq = tm // n_b
        grid = (n_t // tq,)
        x_spec = pl.BlockSpec((n_b, tq, D), lambda i: (0, i, 0))
        o_spec = pl.BlockSpec((tm, D), lambda i: (i, 0))
        out_shape = (n_t * n_b, D)
    elif relayout == "out":
        rows = x.shape[0]
        tq = tm // n_batch
        grid = (rows // tm,)
        x_spec = pl.BlockSpec((tm, D), lambda i: (i, 0))
        o_spec = pl.BlockSpec((n_batch, tq, D), lambda i: (0, i, 0))
        out_shape = (n_batch, rows // n_batch, D)
    else:
        rows = x.shape[0]
        grid = (rows // tm,)
        x_spec = o_spec = pl.BlockSpec((tm, D), lambda i: (i, 0))
        out_shape = (rows, D)
    return pl.pallas_call(
        functools.partial(_ffn_body, g_row=g_row, relayout=relayout),
        grid=grid,
        in_specs=[x_spec, _layer(norm_g.shape[1:], l), _layer(w_in16.shape[1:], l),
                  _layer(w_out16.shape[1:], l)],
        out_specs=o_spec,
        out_shape=jax.ShapeDtypeStruct(out_shape, F32),
        compiler_params=_params(("arbitrary",)),
        name="ffn",
    )(x, norm_g, w_in16, w_out16)


def _memkv_body(m_ref, g_ref, wk_ref, wv_ref, k_ref, v_ref, k16_ref, v16_ref):
    mn = _rms(m_ref[...], g_ref[6:7, :]).astype(BF16)
    k = _dot(mn, wk_ref[...])
    v = _dot(mn, wv_ref[...])
    k_ref[...] = k.reshape(k_ref.shape)
    v_ref[...] = v.reshape(v_ref.shape)
    k16_ref[...] = k.astype(BF16)
    v16_ref[...] = v.astype(BF16)


def _memkv(mem2d, norm_g, wk16, wv16, tm=512):
    rows = mem2d.shape[0]
    w_spec = pl.BlockSpec((None, D, D), lambda l, i: (l, 0, 0))
    o_spec = pl.BlockSpec((None, tm, D), lambda l, i: (l, i, 0))
    o5_spec = pl.BlockSpec((None, tm // N_MEM, N_MEM, HEADS, HEAD_DIM), lambda l, i: (l, i, 0, 0, 0))
    return pl.pallas_call(
        _memkv_body,
        grid=(DEPTH, rows // tm),
        in_specs=[pl.BlockSpec((tm, D), lambda l, i: (i, 0)),
                  pl.BlockSpec((None,) + norm_g.shape[1:], lambda l, i: (l, 0, 0)),
                  w_spec, w_spec],
        out_specs=[o5_spec] * 2 + [o_spec] * 2,
        out_shape=[jax.ShapeDtypeStruct((DEPTH, rows // N_MEM, N_MEM, HEADS, HEAD_DIM), F32)] * 2
                  + [jax.ShapeDtypeStruct((DEPTH, rows, D), BF16)] * 2,
        compiler_params=_params(("arbitrary", "arbitrary")),
        name="memkv",
    )(mem2d, norm_g, wk16, wv16)


def _softmax_rows(s):
    e = jnp.exp(s - jnp.max(s, axis=-1, keepdims=True))
    return e * (1.0 / jnp.sum(e, axis=-1, keepdims=True))


def _xattn_prompt_body(x_ref, k_ref, v_ref, g_ref, wq_ref, wo_ref, o_ref, q_s, o_s, *, nb):
    rows = x_ref.shape[0]
    tt = rows // nb
    n_lb = D // LANES
    lb_per_head = HEAD_DIM // LANES
    x = x_ref[...]
    u = _rms(x, g_ref[G_XA:G_XA + 1, :]).astype(BF16)
    q = _dot(u, wq_ref[...]) * (HEAD_DIM ** -0.5)
    for c in range(n_lb):
        q_s[c] = q[:, c * LANES:(c + 1) * LANES]
    scores = []
    for b in range(nb):
        for h in range(HEADS):
            qbh = jnp.concatenate(
                [q_s[h * lb_per_head + i, pl.ds(b, tt, stride=nb), :] for i in range(lb_per_head)],
                axis=-1).astype(BF16)
            scores.append(lax.dot_general(
                qbh, k_ref[b * N_MEM:(b + 1) * N_MEM, h * HEAD_DIM:(h + 1) * HEAD_DIM],
                (((1,), (1,)), ((), ())), preferred_element_type=F32))
    p = _softmax_rows(jnp.stack(scores)).astype(BF16)
    for b in range(nb):
        for h in range(HEADS):
            obh = _dot(p[b * HEADS + h],
                       v_ref[b * N_MEM:(b + 1) * N_MEM, h * HEAD_DIM:(h + 1) * HEAD_DIM])
            for i in range(lb_per_head):
                o_s[h * lb_per_head + i, pl.ds(b, tt, stride=nb), :] = obh[:, i * LANES:(i + 1) * LANES]
    o = jnp.concatenate([o_s[c] for c in range(n_lb)], axis=-1)
    y = _dot(o.astype(BF16), wo_ref[...])
    o_ref[...] = x + _rms(y, g_ref[G_XA + 1:G_XA + 2, :])


def _xattn_prompt(x, l, norm_g, k16, v16, wq16, wo16, nb, tt):
    rows = tt * nb
    x_spec = pl.BlockSpec((rows, D), lambda i: (i, 0))
    return pl.pallas_call(
        functools.partial(_xattn_prompt_body, nb=nb),
        grid=(x.shape[0] // rows,),
        in_specs=[x_spec, _layer(k16.shape[1:], l), _layer(v16.shape[1:], l),
                  _layer(norm_g.shape[1:], l), _layer(wq16.shape[1:], l), _layer(wo16.shape[1:], l)],
        out_specs=x_spec,
        out_shape=jax.ShapeDtypeStruct(x.shape, F32),
        scratch_shapes=[pltpu.VMEM((D // LANES, rows, LANES), F32)] * 2,
        compiler_params=_params(("arbitrary",)),
        name="xattn_prompt",
    )(x, k16, v16, norm_g, wq16, wo16)


def _xattn_sample_body(x_ref, k_ref, v_ref, g_ref, wq_ref, wo_ref, o_ref, *, bb, tq):
    x = x_ref[...]
    u = _rms(x, g_ref[G_XA:G_XA + 1, :]).astype(BF16)
    q = _dot(u, wq_ref[...]) * (HEAD_DIM ** -0.5)
    n_q, n_k = HEADS * tq, N_MEM * HEADS
    row_head = lax.broadcasted_iota(jnp.int32, (n_q, n_k), 0) // tq
    col_head = lax.broadcasted_iota(jnp.int32, (n_q, n_k), 1) % HEADS
    own_head = row_head == col_head
    outs = []
    for b in range(bb):
        qb = q[b * tq:(b + 1) * tq]
        q2 = jnp.concatenate([qb[:, h * HEAD_DIM:(h + 1) * HEAD_DIM] for h in range(HEADS)], axis=0)
        k2 = k_ref[b].reshape(n_k, HEAD_DIM)
        v2 = v_ref[b].reshape(n_k, HEAD_DIM)
        s = lax.dot_general(q2, k2, (((1,), (1,)), ((), ())), preferred_element_type=F32)
        p = _softmax_rows(jnp.where(own_head, s, jnp.finfo(F32).min))
        o2 = _dot(p, v2)
        outs.append(jnp.concatenate([o2[h * tq:(h + 1) * tq] for h in range(HEADS)], axis=-1))
    o = jnp.concatenate(outs, axis=0)
    y = _dot(o.astype(BF16), wo_ref[...])
    o_ref[...] = x + _rms(y, g_ref[G_XA + 1:G_XA + 2, :])


def _xattn_sample(x, l, norm_g, cache_k, cache_v, wq16, wo16, tq, bb):
    n_batch = cache_k.shape[1]
    x_spec = pl.BlockSpec((bb * tq, D), lambda i: (i, 0))
    kv_spec = pl.BlockSpec((None, bb, N_MEM, HEADS, HEAD_DIM), lambda i: (l, i, 0, 0, 0))
    return pl.pallas_call(
        functools.partial(_xattn_sample_body, bb=bb, tq=tq),
        grid=(n_batch // bb,),
        in_specs=[x_spec, kv_spec, kv_spec, _layer(norm_g.shape[1:], l), _layer(wq16.shape[1:], l),
                  _layer(wo16.shape[1:], l)],
        out_specs=x_spec,
        out_shape=jax.ShapeDtypeStruct(x.shape, F32),
        compiler_params=_params(("arbitrary",)),
        name="xattn_sample",
    )(x, cache_k, cache_v, norm_g, wq16, wo16)


FIR_STEPS = SUBLANES


def _causal_conv(buf_ref, w_ref, b_ref, out_ref, n_taps, tt, nb, side_work=None, side_per_step=0,
                 unrolled=False):
    groups = nb // SUBLANES
    assert groups & (groups - 1) == 0
    n_in = FIR_STEPS + n_taps - 1
    n_steps = (tt // FIR_STEPS) * groups
    lane_blocks = [slice(c * LANES, (c + 1) * LANES) for c in range(D // LANES)]

    def taps(cs):
        return [jnp.broadcast_to(w_ref[k:k + 1, cs], (SUBLANES, LANES)) for k in range(n_taps)]

    def fir_step(s, cs, wk, bias):
        if isinstance(s, int):
            base = (s // groups) * (FIR_STEPS * nb) + (s % groups) * SUBLANES
        else:
            t_chunk = lax.shift_right_logical(s, groups.bit_length() - 1)
            base = t_chunk * (FIR_STEPS * nb) + (s & (groups - 1)) * SUBLANES
            base = pl.multiple_of(base, SUBLANES)
        acc = [bias] * FIR_STEPS
        for j in range(n_in):
            xj = buf_ref[pl.ds(base + j * nb, SUBLANES), cs]
            for o in range(FIR_STEPS):
                if 0 <= j - o < n_taps:
                    acc[o] = acc[o] + wk[j - o] * xj
        for o in range(FIR_STEPS):
            out_ref[pl.ds(base + o * nb, SUBLANES), cs] = acc[o]

    def bias_of(cs):
        return jnp.broadcast_to(b_ref[0:1, cs], (SUBLANES, LANES))

    if unrolled:
        for cs in lane_blocks:
            wk, bias = taps(cs), bias_of(cs)
            for s in range(n_steps):
                fir_step(s, cs, wk, bias)
    elif side_work is None:
        for cs in lane_blocks:
            wk, bias = taps(cs), bias_of(cs)
            lax.fori_loop(0, n_steps, lambda s, carry: (fir_step(s, cs, wk, bias), carry)[1], 0)
    else:
        def body(s, carry):
            for cs in lane_blocks:
                fir_step(s, cs, taps(cs), bias_of(cs))
            for p in range(side_per_step):
                side_work(s * side_per_step + p)
            return carry

        lax.fori_loop(0, n_steps, body, 0)
    return n_steps * side_per_step


def _mixer_body(*refs, tt, nb, has_state):
    if has_state:
        x_ref, ha_ref, hb_ref, h0_ref = refs[:4]
        refs = refs[4:]
    else:
        x_ref = refs[0]
        refs = refs[1:]
    (vec_ref, win_ref, caw_ref, wa_ref, cbw_ref, lruw_ref, wb_ref, wout_ref,
     o_ref, na_ref, nbuf_ref, nh_ref,
     abuf, bbuf, hbuf, conv_s, a_s, u_s, xn_s, proj_s) = refs

    def vec(r, n=1):
        return vec_ref[r:r + n, :]

    rows = tt * nb
    n_steps = (tt // FIR_STEPS) * (nb // SUBLANES)
    n_cc = D // MXU_N
    piece_rows = min(rows, 256)
    row_splits = rows // piece_rows
    ha_rows = (CONV_A - 1) * nb
    hb_rows = (CONV_B - 1) * nb
    ti = pl.program_id(1)

    @pl.when(ti == 0)
    def _():
        if has_state:
            abuf[0:ha_rows, :] = ha_ref[...].reshape(ha_rows, D)
            bbuf[0:hb_rows, :] = hb_ref[...].reshape(hb_rows, D)
            hbuf[...] = h0_ref[...]
        else:
            abuf[0:ha_rows, :] = jnp.zeros((ha_rows, D), F32)
            bbuf[0:hb_rows, :] = jnp.zeros((hb_rows, D), F32)
            hbuf[...] = jnp.zeros((nb, D), F32)

    if has_state:
        x = _to_time_major(x_ref[...].reshape(rows, D), nb, tt)
    else:
        x = x_ref[...]
    u = _rms(x, vec(G_MIX)).astype(BF16)
    xn_s[...] = u

    def proj_now(s):
        cols = [_dot(u, win_ref[s * n_cc + j]) for j in range(n_cc)]
        return jnp.concatenate(cols, axis=-1) + vec(V_BIN + s)

    def proj_piece(p):
        if row_splits == 1:
            q, r0 = p, 0
        else:
            q = lax.shift_right_logical(p, row_splits.bit_length() - 1)
            r0 = pl.multiple_of((p & (row_splits - 1)) * piece_rows, piece_rows)
        proj_s[q, pl.ds(r0, piece_rows), :] = _dot(xn_s[pl.ds(r0, piece_rows), :],
                                                   win_ref[2 * n_cc + q])

    def proj(s):
        cols = [proj_s[(s - 2) * n_cc + j] for j in range(n_cc)]
        return jnp.concatenate(cols, axis=-1) + vec(V_BIN + s)

    abuf[ha_rows:ha_rows + rows, :] = proj_now(0) * _sigmoid(proj_now(1))
    n_pieces = _causal_conv(abuf, caw_ref, vec(V_CONV_A_B), conv_s, CONV_A, tt, nb,
                            side_work=proj_piece, side_per_step=4 * n_cc * row_splits // n_steps)
    assert n_pieces == 4 * n_cc * row_splits
    abuf[0:ha_rows, :] = abuf[rows:rows + ha_rows, :]
    ac = conv_s[...]
    mu = jnp.mean(ac, axis=-1, keepdims=True)
    acc = ac - mu
    var = jnp.mean(acc * acc, axis=-1, keepdims=True)
    ln = acc * lax.rsqrt(var + EPS) * vec(V_LN_G) + vec(V_LN_B)
    y_a = _dot_cols((ln * _sigmoid(ln)).astype(BF16), wa_ref, 0, D)

    bbuf[hb_rows:hb_rows + rows, :] = proj(2)
    _causal_conv(bbuf, cbw_ref, vec(V_CONV_B_B), conv_s, CONV_B, tt, nb, unrolled=True)
    bbuf[0:hb_rows, :] = bbuf[rows:rows + hb_rows, :]
    xb = conv_s[...]
    xb16 = xb.astype(BF16)
    ri = [_dot(xb16[:, n * LRU_BLOCK:(n + 1) * LRU_BLOCK], lruw_ref[n]) for n in range(LRU_BLOCKS)]
    r_pre = jnp.concatenate([p[:, :LRU_BLOCK] for p in ri], axis=-1)
    i_pre = jnp.concatenate([p[:, LRU_BLOCK:] for p in ri], axis=-1)
    r = _sigmoid(r_pre + vec(V_LRU_B_A))
    gi = _sigmoid(i_pre + vec(V_LRU_B_X))
    nlam = -vec(V_LAMBDA)
    softplus = jnp.maximum(nlam, 0.0) + jnp.log1p(jnp.exp(-jnp.abs(nlam)))
    log_a = (-LRU_C) * r * softplus
    a = jnp.exp(log_a)
    one_m_a2 = -jnp.tanh(log_a) * (a * a + 1.0)
    a_s[...] = a
    u_s[...] = jnp.sqrt(one_m_a2) * (gi * xb)

    def scan_step(t, h):
        r0 = pl.multiple_of(t * nb, nb)
        h = a_s[pl.ds(r0, nb), :] * h + u_s[pl.ds(r0, nb), :]
        u_s[pl.ds(r0, nb), :] = h
        return h

    h_last = lax.fori_loop(0, tt, scan_step, hbuf[...])
    hbuf[...] = h_last
    hs = u_s[...]
    bg = proj(3)
    gelu = 0.5 * bg * (1.0 + jnp.tanh(math.sqrt(2.0 / math.pi) * (bg + 0.044715 * (bg * bg * bg))))
    y_b = _dot_cols((hs * gelu).astype(BF16), wb_ref, 0, D)

    merged = _sigmoid(proj(4)) * y_a + _sigmoid(proj(5)) * y_b
    m = _dot_cols(merged.astype(BF16), wout_ref, 0, D)
    y = x + _rms(m, vec(G_MIX + 1))
    if has_state:
        o_ref[...] = _to_batch_major(y, nb, tt).reshape(nb, tt, D)
    else:
        o_ref[...] = y

    @pl.when(ti == pl.num_programs(1) - 1)
    def _():
        na_ref[...] = abuf[0:ha_rows, :].reshape(CONV_A - 1, nb, D)
        nbuf_ref[...] = bbuf[0:hb_rows, :].reshape(CONV_B - 1, nb, D)
        nh_ref[...] = h_last


def _mixer(x, state, l, w, tt, nb):
    has_state = state is not None
    rows = tt * nb
    n_steps = (tt // FIR_STEPS) * (nb // SUBLANES)
    if has_state:
        n_b = x.shape[0]
        grid = (n_b // nb, 1)
        x_spec = pl.BlockSpec((nb, tt, D), lambda j, i: (j, 0, 0))
    else:
        n_b = nb
        grid = (1, x.shape[0] // rows)
        x_spec = pl.BlockSpec((rows, D), lambda j, i: (i, 0))

    def state_spec(n):
        return pl.BlockSpec((n, nb, D), lambda j, i: (0, j, 0), pipeline_mode=pl.Buffered(1))

    h_spec = pl.BlockSpec((nb, D), lambda j, i: (j, 0))
    weights = [w["vecs"], w["w_in"], w["conv_a_w"], w["w_a_out"], w["conv_b_w"], w["lru_w"],
               w["w_b_out"], w["w_out"]]
    state_in = list(state) if has_state else []
    state_specs = [state_spec(CONV_A - 1), state_spec(CONV_B - 1), h_spec]
    state_in_specs = [pl.BlockSpec((None, CONV_A - 1, nb, D), lambda j, i: (l, 0, j, 0),
                                   pipeline_mode=pl.Buffered(1)),
                      pl.BlockSpec((None, CONV_B - 1, nb, D), lambda j, i: (l, 0, j, 0)),
                      pl.BlockSpec((None, nb, D), lambda j, i: (l, j, 0))]
    return pl.pallas_call(
        functools.partial(_mixer_body, tt=tt, nb=nb, has_state=has_state),
        grid=grid,
        in_specs=[x_spec] + (state_in_specs if has_state else [])
                 + [_layer(a.shape[1:], l) for a in weights],
        out_specs=[x_spec] + state_specs,
        out_shape=[jax.ShapeDtypeStruct(x.shape, F32),
                   jax.ShapeDtypeStruct((CONV_A - 1, n_b, D), F32),
                   jax.ShapeDtypeStruct((CONV_B - 1, n_b, D), F32),
                   jax.ShapeDtypeStruct((n_b, D), F32)],
        scratch_shapes=[pltpu.VMEM(((CONV_A - 1) * nb + rows, D), F32),
                        pltpu.VMEM(((CONV_B - 1) * nb + rows, D), F32),
                        pltpu.VMEM((nb, D), F32),
                        pltpu.VMEM((rows, D), F32),
                        pltpu.VMEM((rows, D), F32),
                        pltpu.VMEM((rows, D), F32),
                        pltpu.VMEM((rows, D), BF16),
                        pltpu.VMEM((4 * D // MXU_N, rows, MXU_N), F32)],
        compiler_params=_params(("arbitrary", "arbitrary")),
        name="mixer",
    )(x, *state_in, *weights)


def kernel(x_prompt, x_sample, mem_prompt, cache_mem_k, cache_mem_v, state_conv_a, state_conv_b, state_lru_h, w_in, b_in, conv_a_w, conv_a_b, conv_ln_g, conv_ln_b, w_a_out, conv_b_w, conv_b_b, lru_w_a, lru_b_a, lru_w_x, lru_b_x, lru_lambda, w_b_out, w_out, xa_w_q, xa_w_k, xa_w_v, xa_w_o, ffn1_w_in, ffn1_w_out, ffn2_w_in, ffn2_w_out, norm_g):
    n_b, n_t, _ = x_prompt.shape
    s_b, s_t, _ = x_sample.shape

    def vec(a):
        return a.reshape(DEPTH, 1, a.shape[-1])

    vec_rows = [norm_g, vec(conv_a_b), vec(conv_ln_g), vec(conv_ln_b), vec(conv_b_b), vec(lru_b_a),
                vec(lru_b_x), vec(lru_lambda), b_in.reshape(DEPTH, 6, D)]
    n_rows = sum(a.shape[1] for a in vec_rows)
    vec_rows.append(jnp.zeros((DEPTH, N_VEC_ROWS - n_rows, D), F32))
    mixer_w = dict(
        vecs=jnp.concatenate(vec_rows, axis=1),
        w_in=w_in.reshape(DEPTH, D, 6 * D // MXU_N, MXU_N).transpose(0, 2, 1, 3).astype(BF16),
        conv_a_w=conv_a_w, w_a_out=w_a_out.astype(BF16), conv_b_w=conv_b_w,
        lru_w=jnp.concatenate([lru_w_a, lru_w_x], axis=-1).astype(BF16),
        w_b_out=w_b_out.astype(BF16), w_out=w_out.astype(BF16))
    ffn_w = {G_FFN1: (ffn1_w_in.astype(BF16), ffn1_w_out.astype(BF16)),
             G_FFN2: (ffn2_w_in.astype(BF16), ffn2_w_out.astype(BF16))}
    wq16 = xa_w_q.reshape(DEPTH, D, D).astype(BF16)
    wo16 = xa_w_o.reshape(DEPTH, D, D).astype(BF16)

    mem_k, mem_v, k16, v16 = _memkv(mem_prompt.reshape(n_b * N_MEM, D), norm_g,
                                    xa_w_k.reshape(DEPTH, D, D).astype(BF16),
                                    xa_w_v.reshape(DEPTH, D, D).astype(BF16))

    tm = 512
    p_a, p_b, p_h = [], [], []
    x = x_prompt
    for l in range(DEPTH):
        x = _ffn(x, l, G_FFN1, norm_g, *ffn_w[G_FFN1], tm=tm, relayout="in" if l == 0 else None)
        x, na, nbuf, nh = _mixer(x, None, l, mixer_w, tt=64, nb=n_b)
        p_a.append(na)
        p_b.append(nbuf)
        p_h.append(nh)
        x = _xattn_prompt(x, l, norm_g, k16, v16, wq16, wo16, nb=n_b, tt=128)
        x = _ffn(x, l, G_FFN2, norm_g, *ffn_w[G_FFN2], tm=tm,
                 relayout="out" if l == DEPTH - 1 else None, n_batch=n_b)
    y_prompt = x

    ts = 512
    s_a, s_bb, s_h = [], [], []
    state_tm = (state_conv_a.transpose(0, 2, 1, 3), state_conv_b.transpose(0, 2, 1, 3), state_lru_h)
    x = x_sample.reshape(s_b * s_t, D)
    for l in range(DEPTH):
        x = _ffn(x, l, G_FFN1, norm_g, *ffn_w[G_FFN1], tm=ts)
        x, na, nbuf, nh = _mixer(x.reshape(s_b, s_t, D), state_tm, l, mixer_w, tt=s_t, nb=32)
        s_a.append(na)
        s_bb.append(nbuf)
        s_h.append(nh)
        x = _xattn_sample(x.reshape(s_b * s_t, D), l, norm_g, cache_mem_k, cache_mem_v, wq16, wo16,
                          tq=s_t, bb=8)
        x = _ffn(x, l, G_FFN2, norm_g, *ffn_w[G_FFN2], tm=ts)
    y_sample = x.reshape(s_b, s_t, D)

    def batch_major(states):
        return jnp.stack(states).transpose(0, 2, 1, 3)

    return (y_prompt, y_sample, mem_k, mem_v,
            batch_major(p_a), batch_major(p_b), jnp.stack(p_h),
            batch_major(s_a), batch_major(s_bb), jnp.stack(s_h))
```

```python
import functools
import math

import jax
import jax.numpy as jnp
from jax import lax
from jax.experimental import pallas as pl
from jax.experimental.pallas import tpu as pltpu

D = 1024
DEPTH = 2
N_MEM = 256
HEADS = 4
HEAD_DIM = D // HEADS
CONV_A = 31
CONV_B = 4
LRU_BLOCKS = 8
LRU_BLOCK = D // LRU_BLOCKS
LRU_C = 8.0
D_FF = 2816
EPS = 1e-6
G_FFN1, G_MIX, G_XA, G_FFN2 = 0, 2, 4, 7
V_CONV_A_B, V_LN_G, V_LN_B, V_CONV_B_B, V_LRU_B_A, V_LRU_B_X, V_LAMBDA, V_BIN = 9, 10, 11, 12, 13, 14, 15, 16
N_VEC_ROWS = 24

LANES = 128
SUBLANES = 8
MXU_N = 256
FF_CHUNK = MXU_N
N_FF_CHUNKS = D_FF // FF_CHUNK
VMEM_LIMIT = 56 * 1024 * 1024

PROMPT_ROW_TILE = 512
XATTN_PROMPT_STEPS = 128
SAMPLE_ROW_TILE = 512
MIXER_SAMPLE_BATCH = 32
XATTN_SAMPLE_BATCH = 8

F32 = jnp.float32
BF16 = jnp.bfloat16


def _dot(a, b):
    return jnp.dot(a, b, preferred_element_type=F32)


def _dot_cols(a, w_ref, col0, n_cols):
    return jnp.concatenate(
        [_dot(a, w_ref[:, col0 + j:col0 + j + MXU_N]) for j in range(0, n_cols, MXU_N)], axis=-1)


def _rms(x, g):
    ms = jnp.mean(x * x, axis=-1, keepdims=True)
    return x * lax.rsqrt(ms + EPS) * g


def _sigmoid(x):
    return 0.5 + 0.5 * jnp.tanh(0.5 * x)


def _layer(shape, l):
    nd = len(shape)
    return pl.BlockSpec((None,) + tuple(shape), lambda *_: (l,) + (0,) * nd,
                        pipeline_mode=pl.Buffered(1))


def _params(sem):
    return pltpu.CompilerParams(dimension_semantics=sem, vmem_limit_bytes=VMEM_LIMIT)


def _to_time_major(v, nb, tt):
    return jnp.swapaxes(v.reshape(nb, tt, D), 0, 1).reshape(tt * nb, D)


def _to_batch_major(v, nb, tt):
    return jnp.swapaxes(v.reshape(tt, nb, D), 0, 1).reshape(nb * tt, D)


def _ffn_body(x_ref, g_ref, win_ref, wout_ref, o_ref, *, g_row, relayout):
    if relayout == "in":
        nb, tq, _ = x_ref.shape
        x = x_ref[...].reshape(nb * tq, D)
    else:
        x = x_ref[...]
    xn = _rms(x, g_ref[g_row:g_row + 1, :]).astype(BF16)
    acc = None
    for j in range(N_FF_CHUNKS):
        lo = j * FF_CHUNK
        gate = _dot(xn, win_ref[:, lo:lo + FF_CHUNK])
        up = _dot(xn, win_ref[:, D_FF + lo:D_FF + lo + FF_CHUNK])
        h = (gate * _sigmoid(gate) * up).astype(BF16)
        d = _dot(h, wout_ref[lo:lo + FF_CHUNK, :])
        acc = d if acc is None else acc + d
    y = x + 0.5 * _rms(acc, g_ref[g_row + 1:g_row + 2, :])
    if relayout == "in":
        o_ref[...] = _to_time_major(y, nb, tq)
    elif relayout == "out":
        nb, tq, _ = o_ref.shape
        o_ref[...] = _to_batch_major(y, nb, tq).reshape(nb, tq, D)
    else:
        o_ref[...] = y


def _ffn(x, l, g_row, norm_g, w_in16, w_out16, tm, relayout=None, n_batch=None):
    if relayout == "in":
        n_b, n_t, _ = x.shape
        tq = tm // n_b
        grid = (n_t // tq,)
        x_spec = pl.BlockSpec((n_b, tq, D), lambda i: (0, i, 0))
        o_spec = pl.BlockSpec((tm, D), lambda i: (i, 0))
        out_shape = (n_t * n_b, D)
    elif relayout == "out":
        rows = x.shape[0]
        tq = tm // n_batch
        grid = (rows // tm,)
        x_spec = pl.BlockSpec((tm, D), lambda i: (i, 0))
        o_spec = pl.BlockSpec((n_batch, tq, D), lambda i: (0, i, 0))
        out_shape = (n_batch, rows // n_batch, D)
    else:
        rows = x.shape[0]
        grid = (rows // tm,)
        x_spec = o_spec = pl.BlockSpec((tm, D), lambda i: (i, 0))
        out_shape = (rows, D)
    return pl.pallas_call(
        functools.partial(_ffn_body, g_row=g_row, relayout=relayout),
        grid=grid,
        in_specs=[x_spec, _layer(norm_g.shape[1:], l), _layer(w_in16.shape[1:], l),
                  _layer(w_out16.shape[1:], l)],
        out_specs=o_spec,
        out_shape=jax.ShapeDtypeStruct(out_shape, F32),
        compiler_params=_params(("arbitrary",)),
        name="ffn",
    )(x, norm_g, w_in16, w_out16)


def _memkv_body(m_ref, g_ref, wk_ref, wv_ref, k_ref, v_ref, k16_ref, v16_ref):
    mn = _rms(m_ref[...], g_ref[6:7, :]).astype(BF16)
    k = _dot(mn, wk_ref[...])
    v = _dot(mn, wv_ref[...])
    k_ref[...] = k.reshape(k_ref.shape)
    v_ref[...] = v.reshape(v_ref.shape)
    k16_ref[...] = k.astype(BF16)
    v16_ref[...] = v.astype(BF16)


def _memkv(mem2d, norm_g, wk16, wv16, tm=512):
    rows = mem2d.shape[0]
    w_spec = pl.BlockSpec((None, D, D), lambda l, i: (l, 0, 0))
    o_spec = pl.BlockSpec((None, tm, D), lambda l, i: (l, i, 0))
    o5_spec = pl.BlockSpec((None, tm // N_MEM, N_MEM, HEADS, HEAD_DIM), lambda l, i: (l, i, 0, 0, 0))
    return pl.pallas_call(
        _memkv_body,
        grid=(DEPTH, rows // tm),
        in_specs=[pl.BlockSpec((tm, D), lambda l, i: (i, 0)),
                  pl.BlockSpec((None,) + norm_g.shape[1:], lambda l, i: (l, 0, 0)),
                  w_spec, w_spec],
        out_specs=[o5_spec] * 2 + [o_spec] * 2,
        out_shape=[jax.ShapeDtypeStruct((DEPTH, rows // N_MEM, N_MEM, HEADS, HEAD_DIM), F32)] * 2
                  + [jax.ShapeDtypeStruct((DEPTH, rows, D), BF16)] * 2,
        compiler_params=_params(("arbitrary", "arbitrary")),
        name="memkv",
    )(mem2d, norm_g, wk16, wv16)


def _softmax_rows(s):
    e = jnp.exp(s - jnp.max(s, axis=-1, keepdims=True))
    return e * (1.0 / jnp.sum(e, axis=-1, keepdims=True))


def _xattn_prompt_body(x_ref, k_ref, v_ref, g_ref, wq_ref, wo_ref, o_ref, q_s, o_s, *, nb):
    rows = x_ref.shape[0]
    tt = rows // nb
    n_lb = D // LANES
    lb_per_head = HEAD_DIM // LANES
    x = x_ref[...]
    u = _rms(x, g_ref[G_XA:G_XA + 1, :]).astype(BF16)
    q = _dot(u, wq_ref[...]) * (HEAD_DIM ** -0.5)
    for c in range(n_lb):
        q_s[c] = q[:, c * LANES:(c + 1) * LANES]
    scores = []
    for b in range(nb):
        for h in range(HEADS):
            qbh = jnp.concatenate(
                [q_s[h * lb_per_head + i, pl.ds(b, tt, stride=nb), :] for i in range(lb_per_head)],
                axis=-1).astype(BF16)
            scores.append(lax.dot_general(
                qbh, k_ref[b * N_MEM:(b + 1) * N_MEM, h * HEAD_DIM:(h + 1) * HEAD_DIM],
                (((1,), (1,)), ((), ())), preferred_element_type=F32))
    p = _softmax_rows(jnp.stack(scores)).astype(BF16)
    for b in range(nb):
        for h in range(HEADS):
            obh = _dot(p[b * HEADS + h],
                       v_ref[b * N_MEM:(b + 1) * N_MEM, h * HEAD_DIM:(h + 1) * HEAD_DIM])
            for i in range(lb_per_head):
                o_s[h * lb_per_head + i, pl.ds(b, tt, stride=nb), :] = obh[:, i * LANES:(i + 1) * LANES]
    o = jnp.concatenate([o_s[c] for c in range(n_lb)], axis=-1)
    y = _dot(o.astype(BF16), wo_ref[...])
    o_ref[...] = x + _rms(y, g_ref[G_XA + 1:G_XA + 2, :])


def _xattn_prompt(x, l, norm_g, k16, v16, wq16, wo16, nb, tt):
    rows = tt * nb
    x_spec = pl.BlockSpec((rows, D), lambda i: (i, 0))
    return pl.pallas_call(
        functools.partial(_xattn_prompt_body, nb=nb),
        grid=(x.shape[0] // rows,),
        in_specs=[x_spec, _layer(k16.shape[1:], l), _layer(v16.shape[1:], l),
                  _layer(norm_g.shape[1:], l), _layer(wq16.shape[1:], l), _layer(wo16.shape[1:], l)],
        out_specs=x_spec,
        out_shape=jax.ShapeDtypeStruct(x.shape, F32),
        scratch_shapes=[pltpu.VMEM((D // LANES, rows, LANES), F32)] * 2,
        compiler_params=_params(("arbitrary",)),
        name="xattn_prompt",
    )(x, k16, v16, norm_g, wq16, wo16)


def _xattn_sample_body(x_ref, k_ref, v_ref, g_ref, wq_ref, wo_ref, o_ref, *, bb, tq):
    x = x_ref[...]
    u = _rms(x, g_ref[G_XA:G_XA + 1, :]).astype(BF16)
    q = _dot(u, wq_ref[...]) * (HEAD_DIM ** -0.5)
    n_q, n_k = HEADS * tq, N_MEM * HEADS
    row_head = lax.broadcasted_iota(jnp.int32, (n_q, n_k), 0) // tq
    col_head = lax.broadcasted_iota(jnp.int32, (n_q, n_k), 1) % HEADS
    own_head = row_head == col_head
    outs = []
    for b in range(bb):
        qb = q[b * tq:(b + 1) * tq]
        q2 = jnp.concatenate([qb[:, h * HEAD_DIM:(h + 1) * HEAD_DIM] for h in range(HEADS)], axis=0)
        k2 = k_ref[b].reshape(n_k, HEAD_DIM)
        v2 = v_ref[b].reshape(n_k, HEAD_DIM)
        s = lax.dot_general(q2, k2, (((1,), (1,)), ((), ())), preferred_element_type=F32)
        p = _softmax_rows(jnp.where(own_head, s, jnp.finfo(F32).min))
        o2 = _dot(p, v2)
        outs.append(jnp.concatenate([o2[h * tq:(h + 1) * tq] for h in range(HEADS)], axis=-1))
    o = jnp.concatenate(outs, axis=0)
    y = _dot(o.astype(BF16), wo_ref[...])
    o_ref[...] = x + _rms(y, g_ref[G_XA + 1:G_XA + 2, :])


def _xattn_sample(x, l, norm_g, cache_k, cache_v, wq16, wo16, tq, bb):
    n_batch = cache_k.shape[1]
    x_spec = pl.BlockSpec((bb * tq, D), lambda i: (i, 0))
    kv_spec = pl.BlockSpec((None, bb, N_MEM, HEADS, HEAD_DIM), lambda i: (l, i, 0, 0, 0))
    return pl.pallas_call(
        functools.partial(_xattn_sample_body, bb=bb, tq=tq),
        grid=(n_batch // bb,),
        in_specs=[x_spec, kv_spec, kv_spec, _layer(norm_g.shape[1:], l), _layer(wq16.shape[1:], l),
                  _layer(wo16.shape[1:], l)],
        out_specs=x_spec,
        out_shape=jax.ShapeDtypeStruct(x.shape, F32),
        compiler_params=_params(("arbitrary",)),
        name="xattn_sample",
    )(x, cache_k, cache_v, norm_g, wq16, wo16)


FIR_STEPS = SUBLANES


def _causal_conv(buf_ref, w_ref, bias_row, out_ref, n_taps, tt, nb, unrolled=False):
    groups = nb // SUBLANES
    assert groups & (groups - 1) == 0
    n_in = FIR_STEPS + n_taps - 1
    n_steps = (tt // FIR_STEPS) * groups
    lane_blocks = [slice(c * LANES, (c + 1) * LANES) for c in range(D // LANES)]

    def taps(cs):
        return [jnp.broadcast_to(w_ref[k:k + 1, cs], (SUBLANES, LANES)) for k in range(n_taps)]

    def fir_step(s, cs, wk, bias):
        if isinstance(s, int):
            base = (s // groups) * (FIR_STEPS * nb) + (s % groups) * SUBLANES
        else:
            t_chunk = lax.shift_right_logical(s, groups.bit_length() - 1)
            base = t_chunk * (FIR_STEPS * nb) + (s & (groups - 1)) * SUBLANES
            base = pl.multiple_of(base, SUBLANES)
        acc = [bias] * FIR_STEPS
        for j in range(n_in):
            xj = buf_ref[pl.ds(base + j * nb, SUBLANES), cs]
            for o in range(FIR_STEPS):
                if 0 <= j - o < n_taps:
                    acc[o] = acc[o] + wk[j - o] * xj
        for o in range(FIR_STEPS):
            out_ref[pl.ds(base + o * nb, SUBLANES), cs] = acc[o]

    for cs in lane_blocks:
        wk = taps(cs)
        bias = jnp.broadcast_to(bias_row[0:1, cs], (SUBLANES, LANES))
        if unrolled:
            for s in range(n_steps):
                fir_step(s, cs, wk, bias)
        else:
            lax.fori_loop(0, n_steps, lambda s, carry: (fir_step(s, cs, wk, bias), carry)[1], 0)


def _mixer_body(*refs, tt, nb, has_state):
    if has_state:
        x_ref, ha_ref, hb_ref, h0_ref = refs[:4]
        refs = refs[4:]
    else:
        x_ref = refs[0]
        refs = refs[1:]
    (vec_ref, win_ref, caw_ref, wa_ref, cbw_ref, lruw_ref, wb_ref, wout_ref,
     o_ref, na_ref, nbuf_ref, nh_ref,
     abuf, bbuf, hbuf, conv_s, a_s, u_s) = refs

    def vec(r):
        return vec_ref[r:r + 1, :]

    rows = tt * nb
    ha_rows = (CONV_A - 1) * nb
    hb_rows = (CONV_B - 1) * nb
    ti = pl.program_id(1)

    @pl.when(ti == 0)
    def _():
        if has_state:
            abuf[0:ha_rows, :] = ha_ref[...].reshape(ha_rows, D)
            bbuf[0:hb_rows, :] = hb_ref[...].reshape(hb_rows, D)
            hbuf[...] = h0_ref[...]
        else:
            abuf[0:ha_rows, :] = jnp.zeros((ha_rows, D), F32)
            bbuf[0:hb_rows, :] = jnp.zeros((hb_rows, D), F32)
            hbuf[...] = jnp.zeros((nb, D), F32)

    if has_state:
        x = _to_time_major(x_ref[...].reshape(rows, D), nb, tt)
    else:
        x = x_ref[...]
    u = _rms(x, vec(G_MIX)).astype(BF16)

    def proj(s):
        return _dot_cols(u, win_ref, s * D, D) + vec(V_BIN + s)

    abuf[ha_rows:ha_rows + rows, :] = proj(0) * _sigmoid(proj(1))
    _causal_conv(abuf, caw_ref, vec(V_CONV_A_B), conv_s, CONV_A, tt, nb)
    abuf[0:ha_rows, :] = abuf[rows:rows + ha_rows, :]
    ac = conv_s[...]
    mu = jnp.mean(ac, axis=-1, keepdims=True)
    acc = ac - mu
    var = jnp.mean(acc * acc, axis=-1, keepdims=True)
    ln = acc * lax.rsqrt(var + EPS) * vec(V_LN_G) + vec(V_LN_B)
    y_a = _dot_cols((ln * _sigmoid(ln)).astype(BF16), wa_ref, 0, D)

    bbuf[hb_rows:hb_rows + rows, :] = proj(2)
    _causal_conv(bbuf, cbw_ref, vec(V_CONV_B_B), conv_s, CONV_B, tt, nb, unrolled=True)
    bbuf[0:hb_rows, :] = bbuf[rows:rows + hb_rows, :]
    xb = conv_s[...]
    xb16 = xb.astype(BF16)
    ri = [_dot(xb16[:, n * LRU_BLOCK:(n + 1) * LRU_BLOCK], lruw_ref[n]) for n in range(LRU_BLOCKS)]
    r_pre = jnp.concatenate([p[:, :LRU_BLOCK] for p in ri], axis=-1)
    i_pre = jnp.concatenate([p[:, LRU_BLOCK:] for p in ri], axis=-1)
    r = _sigmoid(r_pre + vec(V_LRU_B_A))
    gi = _sigmoid(i_pre + vec(V_LRU_B_X))
    nlam = -vec(V_LAMBDA)
    softplus = jnp.maximum(nlam, 0.0) + jnp.log1p(jnp.exp(-jnp.abs(nlam)))
    log_a = (-LRU_C) * r * softplus
    a = jnp.exp(log_a)
    one_m_a2 = -jnp.tanh(log_a) * (a * a + 1.0)
    a_s[...] = a
    u_s[...] = jnp.sqrt(one_m_a2) * (gi * xb)

    def scan_step(t, h):
        r0 = pl.multiple_of(t * nb, nb)
        h = a_s[pl.ds(r0, nb), :] * h + u_s[pl.ds(r0, nb), :]
        u_s[pl.ds(r0, nb), :] = h
        return h

    h_last = lax.fori_loop(0, tt, scan_step, hbuf[...])
    hbuf[...] = h_last
    hs = u_s[...]
    bg = proj(3)
    gelu = 0.5 * bg * (1.0 + jnp.tanh(math.sqrt(2.0 / math.pi) * (bg + 0.044715 * (bg * bg * bg))))
    y_b = _dot_cols((hs * gelu).astype(BF16), wb_ref, 0, D)

    merged = _sigmoid(proj(4)) * y_a + _sigmoid(proj(5)) * y_b
    m = _dot_cols(merged.astype(BF16), wout_ref, 0, D)
    y = x + _rms(m, vec(G_MIX + 1))
    if has_state:
        o_ref[...] = _to_batch_major(y, nb, tt).reshape(nb, tt, D)
    else:
        o_ref[...] = y

    @pl.when(ti == pl.num_programs(1) - 1)
    def _():
        na_ref[...] = abuf[0:ha_rows, :].reshape(CONV_A - 1, nb, D)
        nbuf_ref[...] = bbuf[0:hb_rows, :].reshape(CONV_B - 1, nb, D)
        nh_ref[...] = h_last


def _mixer(x, state, l, w, tt, nb):
    has_state = state is not None
    rows = tt * nb
    if has_state:
        n_b = x.shape[0]
        grid = (n_b // nb, 1)
        x_spec = pl.BlockSpec((nb, tt, D), lambda j, i: (j, 0, 0))
    else:
        n_b = nb
        grid = (1, x.shape[0] // rows)
        x_spec = pl.BlockSpec((rows, D), lambda j, i: (i, 0))

    def state_spec(n):
        return pl.BlockSpec((n, nb, D), lambda j, i: (0, j, 0), pipeline_mode=pl.Buffered(1))

    h_spec = pl.BlockSpec((nb, D), lambda j, i: (j, 0))
    weights = [w["vecs"], w["w_in"], w["conv_a_w"], w["w_a_out"], w["conv_b_w"], w["lru_w"],
               w["w_b_out"], w["w_out"]]
    state_in = list(state) if has_state else []
    state_specs = [state_spec(CONV_A - 1), state_spec(CONV_B - 1), h_spec]
    state_in_specs = [pl.BlockSpec((None, CONV_A - 1, nb, D), lambda j, i: (l, 0, j, 0),
                                   pipeline_mode=pl.Buffered(1)),
                      pl.BlockSpec((None, CONV_B - 1, nb, D), lambda j, i: (l, 0, j, 0)),
                      pl.BlockSpec((None, nb, D), lambda j, i: (l, j, 0))]
    return pl.pallas_call(
        functools.partial(_mixer_body, tt=tt, nb=nb, has_state=has_state),
        grid=grid,
        in_specs=[x_spec] + (state_in_specs if has_state else [])
                 + [_layer(a.shape[1:], l) for a in weights],
        out_specs=[x_spec] + state_specs,
        out_shape=[jax.ShapeDtypeStruct(x.shape, F32),
                   jax.ShapeDtypeStruct((CONV_A - 1, n_b, D), F32),
                   jax.ShapeDtypeStruct((CONV_B - 1, n_b, D), F32),
                   jax.ShapeDtypeStruct((n_b, D), F32)],
        scratch_shapes=[pltpu.VMEM(((CONV_A - 1) * nb + rows, D), F32),
                        pltpu.VMEM(((CONV_B - 1) * nb + rows, D), F32),
                        pltpu.VMEM((nb, D), F32),
                        pltpu.VMEM((rows, D), F32),
                        pltpu.VMEM((rows, D), F32),
                        pltpu.VMEM((rows, D), F32)],
        compiler_params=_params(("arbitrary", "arbitrary")),
        name="mixer",
    )(x, *state_in, *weights)


def kernel(x_prompt, x_sample, mem_prompt, cache_mem_k, cache_mem_v, state_conv_a, state_conv_b, state_lru_h, w_in, b_in, conv_a_w, conv_a_b, conv_ln_g, conv_ln_b, w_a_out, conv_b_w, conv_b_b, lru_w_a, lru_b_a, lru_w_x, lru_b_x, lru_lambda, w_b_out, w_out, xa_w_q, xa_w_k, xa_w_v, xa_w_o, ffn1_w_in, ffn1_w_out, ffn2_w_in, ffn2_w_out, norm_g):
    n_b, n_t, _ = x_prompt.shape
    s_b, s_t, _ = x_sample.shape

    def vec(a):
        return a.reshape(DEPTH, 1, a.shape[-1])

    vec_rows = [norm_g, vec(conv_a_b), vec(conv_ln_g), vec(conv_ln_b), vec(conv_b_b), vec(lru_b_a),
                vec(lru_b_x), vec(lru_lambda), b_in.reshape(DEPTH, 6, D)]
    n_rows = sum(a.shape[1] for a in vec_rows)
    vec_rows.append(jnp.zeros((DEPTH, N_VEC_ROWS - n_rows, D), F32))
    mixer_w = dict(
        vecs=jnp.concatenate(vec_rows, axis=1), w_in=w_in.astype(BF16), conv_a_w=conv_a_w,
        w_a_out=w_a_out.astype(BF16), conv_b_w=conv_b_w,
        lru_w=jnp.concatenate([lru_w_a, lru_w_x], axis=-1).astype(BF16),
        w_b_out=w_b_out.astype(BF16), w_out=w_out.astype(BF16))
    ffn_w = {G_FFN1: (ffn1_w_in.astype(BF16), ffn1_w_out.astype(BF16)),
             G_FFN2: (ffn2_w_in.astype(BF16), ffn2_w_out.astype(BF16))}
    wq16 = xa_w_q.reshape(DEPTH, D, D).astype(BF16)
    wo16 = xa_w_o.reshape(DEPTH, D, D).astype(BF16)

    mem_k, mem_v, k16, v16 = _memkv(mem_prompt.reshape(n_b * N_MEM, D), norm_g,
                                    xa_w_k.reshape(DEPTH, D, D).astype(BF16),
                                    xa_w_v.reshape(DEPTH, D, D).astype(BF16))

    tm = PROMPT_ROW_TILE
    p_a, p_b, p_h = [], [], []
    x = x_prompt
    for l in range(DEPTH):
        x = _ffn(x, l, G_FFN1, norm_g, *ffn_w[G_FFN1], tm=tm, relayout="in" if l == 0 else None)
        x, na, nbuf, nh = _mixer(x, None, l, mixer_w, tt=tm // n_b, nb=n_b)
        p_a.append(na)
        p_b.append(nbuf)
        p_h.append(nh)
        x = _xattn_prompt(x, l, norm_g, k16, v16, wq16, wo16, nb=n_b, tt=XATTN_PROMPT_STEPS)
        x = _ffn(x, l, G_FFN2, norm_g, *ffn_w[G_FFN2], tm=tm,
                 relayout="out" if l == DEPTH - 1 else None, n_batch=n_b)
    y_prompt = x

    ts = SAMPLE_ROW_TILE
    s_a, s_bb, s_h = [], [], []
    state_tm = (state_conv_a.transpose(0, 2, 1, 3), state_conv_b.transpose(0, 2, 1, 3), state_lru_h)
    x = x_sample.reshape(s_b * s_t, D)
    for l in range(DEPTH):
        x = _ffn(x, l, G_FFN1, norm_g, *ffn_w[G_FFN1], tm=ts)
        x, na, nbuf, nh = _mixer(x.reshape(s_b, s_t, D), state_tm, l, mixer_w, tt=s_t,
                                 nb=MIXER_SAMPLE_BATCH)
        s_a.append(na)
        s_bb.append(nbuf)
        s_h.append(nh)
        x = _xattn_sample(x.reshape(s_b * s_t, D), l, norm_g, cache_mem_k, cache_mem_v, wq16, wo16,
                          tq=s_t, bb=XATTN_SAMPLE_BATCH)
        x = _ffn(x, l, G_FFN2, norm_g, *ffn_w[G_FFN2], tm=ts)
    y_sample = x.reshape(s_b, s_t, D)

    def batch_major(states):
        return jnp.stack(states).transpose(0, 2, 1, 3)

    return (y_prompt, y_sample, mem_k, mem_v,
            batch_major(p_a), batch_major(p_b), jnp.stack(p_h),
            batch_major(s_a), batch_major(s_bb), jnp.stack(s_h))
```

```python
import functools
import math

import jax
import jax.numpy as jnp
from jax import lax
from jax.experimental import pallas as pl
from jax.experimental.pallas import tpu as pltpu

D = 1024
DEPTH = 2
N_MEM = 256
HEADS = 4
HEAD_DIM = D // HEADS
CONV_A = 31
CONV_B = 4
LRU_BLOCKS = 8
LRU_BLOCK = D // LRU_BLOCKS
LRU_C = 8.0
D_FF = 2816
EPS = 1e-6
G_FFN1, G_MIX, G_XA, G_FFN2 = 0, 2, 4, 7
V_CONV_A_B, V_LN_G, V_LN_B, V_CONV_B_B, V_LRU_B_A, V_LRU_B_X, V_LAMBDA, V_BIN = 9, 10, 11, 12, 13, 14, 15, 16
N_VEC_ROWS = 24

LANES = 128
SUBLANES = 8
MXU_N = 256
FF_CHUNK = MXU_N
N_FF_CHUNKS = D_FF // FF_CHUNK
VMEM_LIMIT = 56 * 1024 * 1024

PROMPT_ROW_TILE = 512
XATTN_PROMPT_STEPS = 128
SAMPLE_ROW_TILE = 512
MIXER_SAMPLE_BATCH = 32
XATTN_SAMPLE_BATCH = 8

F32 = jnp.float32
BF16 = jnp.bfloat16


def _dot(a, b):
    return jnp.dot(a, b, preferred_element_type=F32)


def _dot_cols(a, w_ref, col0, n_cols):
    return jnp.concatenate(
        [_dot(a, w_ref[:, col0 + j:col0 + j + MXU_N]) for j in range(0, n_cols, MXU_N)], axis=-1)


def _rms(x, g):
    ms = jnp.mean(x * x, axis=-1, keepdims=True)
    return x * lax.rsqrt(ms + EPS) * g


def _sigmoid(x):
    return 0.5 + 0.5 * jnp.tanh(0.5 * x)


def _layer(shape, l):
    nd = len(shape)
    return pl.BlockSpec((None,) + tuple(shape), lambda *_: (l,) + (0,) * nd,
                        pipeline_mode=pl.Buffered(1))


def _params(sem):
    return pltpu.CompilerParams(dimension_semantics=sem, vmem_limit_bytes=VMEM_LIMIT)


def _to_time_major(v, nb, tt):
    return jnp.swapaxes(v.reshape(nb, tt, D), 0, 1).reshape(tt * nb, D)


def _to_batch_major(v, nb, tt):
    return jnp.swapaxes(v.reshape(tt, nb, D), 0, 1).reshape(nb * tt, D)


def _ffn_body(x_ref, g_ref, win_ref, wout_ref, o_ref, *, g_row, relayout):
    if relayout == "in":
        nb, tq, _ = x_ref.shape
        x = x_ref[...].reshape(nb * tq, D)
    else:
        x = x_ref[...]
    xn = _rms(x, g_ref[g_row:g_row + 1, :]).astype(BF16)
    acc = None
    for j in range(N_FF_CHUNKS):
        lo = j * FF_CHUNK
        gate = _dot(xn, win_ref[:, lo:lo + FF_CHUNK])
        up = _dot(xn, win_ref[:, D_FF + lo:D_FF + lo + FF_CHUNK])
        h = (gate * _sigmoid(gate) * up).astype(BF16)
        d = _dot(h, wout_ref[lo:lo + FF_CHUNK, :])
        acc = d if acc is None else acc + d
    y = x + 0.5 * _rms(acc, g_ref[g_row + 1:g_row + 2, :])
    if relayout == "in":
        o_ref[...] = _to_time_major(y, nb, tq)
    elif relayout == "out":
        nb, tq, _ = o_ref.shape
        o_ref[...] = _to_batch_major(y, nb, tq).reshape(nb, tq, D)
    else:
        o_ref[...] = y


def _ffn(x, l, g_row, norm_g, w_in16, w_out16, tm, relayout=None, n_batch=None):
    if relayout == "in":
        n_b, n_t, _ = x.shape
        tq = tm // n_b
        grid = (n_t // tq,)
        x_spec = pl.BlockSpec((n_b, tq, D), lambda i: (0, i, 0))
        o_spec = pl.BlockSpec((tm, D), lambda i: (i, 0))
        out_shape = (n_t * n_b, D)
    elif relayout == "out":
        rows = x.shape[0]
        tq = tm // n_batch
        grid = (rows // tm,)
        x_spec = pl.BlockSpec((tm, D), lambda i: (i, 0))
        o_spec = pl.BlockSpec((n_batch, tq, D), lambda i: (0, i, 0))
        out_shape = (n_batch, rows // n_batch, D)
    else:
        rows = x.shape[0]
        grid = (rows // tm,)
        x_spec = o_spec = pl.BlockSpec((tm, D), lambda i: (i, 0))
        out_shape = (rows, D)
    return pl.pallas_call(
        functools.partial(_ffn_body, g_row=g_row, relayout=relayout),
        grid=grid,
        in_specs=[x_spec, _layer(norm_g.shape[1:], l), _layer(w_in16.shape[1:], l),
                  _layer(w_out16.shape[1:], l)],
        out_specs=o_spec,
        out_shape=jax.ShapeDtypeStruct(out_shape, F32),
        compiler_params=_params(("arbitrary",)),
        name="ffn",
    )(x, norm_g, w_in16, w_out16)


def _memkv_body(m_ref, g_ref, wk_ref, wv_ref, k_ref, v_ref, k16_ref, v16_ref):
    mn = _rms(m_ref[...], g_ref[6:7, :]).astype(BF16)
    k = _dot(mn, wk_ref[...])
    v = _dot(mn, wv_ref[...])
    k_ref[...] = k.reshape(k_ref.shape)
    v_ref[...] = v.reshape(v_ref.shape)
    k16_ref[...] = k.astype(BF16)
    v16_ref[...] = v.astype(BF16)


def _memkv(mem2d, norm_g, wk16, wv16, tm=512):
    rows = mem2d.shape[0]
    w_spec = pl.BlockSpec((None, D, D), lambda l, i: (l, 0, 0))
    o_spec = pl.BlockSpec((None, tm, D), lambda l, i: (l, i, 0))
    o5_spec = pl.BlockSpec((None, tm // N_MEM, N_MEM, HEADS, HEAD_DIM), lambda l, i: (l, i, 0, 0, 0))
    return pl.pallas_call(
        _memkv_body,
        grid=(DEPTH, rows // tm),
        in_specs=[pl.BlockSpec((tm, D), lambda l, i: (i, 0)),
                  pl.BlockSpec((None,) + norm_g.shape[1:], lambda l, i: (l, 0, 0)),
                  w_spec, w_spec],
        out_specs=[o5_spec] * 2 + [o_spec] * 2,
        out_shape=[jax.ShapeDtypeStruct((DEPTH, rows // N_MEM, N_MEM, HEADS, HEAD_DIM), F32)] * 2
                  + [jax.ShapeDtypeStruct((DEPTH, rows, D), BF16)] * 2,
        compiler_params=_params(("arbitrary", "arbitrary")),
        name="memkv",
    )(mem2d, norm_g, wk16, wv16)


def _softmax_rows(s):
    e = jnp.exp(s - jnp.max(s, axis=-1, keepdims=True))
    return e * (1.0 / jnp.sum(e, axis=-1, keepdims=True))


def _xattn_prompt_body(x_ref, k_ref, v_ref, g_ref, wq_ref, wo_ref, o_ref, q_s, o_s, *, nb):
    rows = x_ref.shape[0]
    tt = rows // nb
    n_lb = D // LANES
    lb_per_head = HEAD_DIM // LANES
    x = x_ref[...]
    u = _rms(x, g_ref[G_XA:G_XA + 1, :]).astype(BF16)
    q = _dot(u, wq_ref[...]) * (HEAD_DIM ** -0.5)
    for c in range(n_lb):
        q_s[c] = q[:, c * LANES:(c + 1) * LANES]
    scores = []
    for b in range(nb):
        for h in range(HEADS):
            qbh = jnp.concatenate(
                [q_s[h * lb_per_head + i, pl.ds(b, tt, stride=nb), :] for i in range(lb_per_head)],
                axis=-1).astype(BF16)
            scores.append(lax.dot_general(
                qbh, k_ref[b * N_MEM:(b + 1) * N_MEM, h * HEAD_DIM:(h + 1) * HEAD_DIM],
                (((1,), (1,)), ((), ())), preferred_element_type=F32))
    p = _softmax_rows(jnp.stack(scores)).astype(BF16)
    for b in range(nb):
        for h in range(HEADS):
            obh = _dot(p[b * HEADS + h],
                       v_ref[b * N_MEM:(b + 1) * N_MEM, h * HEAD_DIM:(h + 1) * HEAD_DIM])
            for i in range(lb_per_head):
                o_s[h * lb_per_head + i, pl.ds(b, tt, stride=nb), :] = obh[:, i * LANES:(i + 1) * LANES]
    o = jnp.concatenate([o_s[c] for c in range(n_lb)], axis=-1)
    y = _dot(o.astype(BF16), wo_ref[...])
    o_ref[...] = x + _rms(y, g_ref[G_XA + 1:G_XA + 2, :])


def _xattn_prompt(x, l, norm_g, k16, v16, wq16, wo16, nb, tt):
    rows = tt * nb
    x_spec = pl.BlockSpec((rows, D), lambda i: (i, 0))
    return pl.pallas_call(
        functools.partial(_xattn_prompt_body, nb=nb),
        grid=(x.shape[0] // rows,),
        in_specs=[x_spec, _layer(k16.shape[1:], l), _layer(v16.shape[1:], l),
                  _layer(norm_g.shape[1:], l), _layer(wq16.shape[1:], l), _layer(wo16.shape[1:], l)],
        out_specs=x_spec,
        out_shape=jax.ShapeDtypeStruct(x.shape, F32),
        scratch_shapes=[pltpu.VMEM((D // LANES, rows, LANES), F32)] * 2,
        compiler_params=_params(("arbitrary",)),
        name="xattn_prompt",
    )(x, k16, v16, norm_g, wq16, wo16)


def _xattn_sample_body(x_ref, k_ref, v_ref, g_ref, wq_ref, wo_ref, o_ref, *, bb, tq):
    x = x_ref[...]
    u = _rms(x, g_ref[G_XA:G_XA + 1, :]).astype(BF16)
    q = _dot(u, wq_ref[...]) * (HEAD_DIM ** -0.5)
    n_q, n_k = HEADS * tq, N_MEM * HEADS
    row_head = lax.broadcasted_iota(jnp.int32, (n_q, n_k), 0) // tq
    col_head = lax.broadcasted_iota(jnp.int32, (n_q, n_k), 1) % HEADS
    own_head = row_head == col_head
    outs = []
    for b in range(bb):
        qb = q[b * tq:(b + 1) * tq]
        q2 = jnp.concatenate([qb[:, h * HEAD_DIM:(h + 1) * HEAD_DIM] for h in range(HEADS)], axis=0)
        k2 = k_ref[b].reshape(n_k, HEAD_DIM)
        v2 = v_ref[b].reshape(n_k, HEAD_DIM)
        s = lax.dot_general(q2, k2, (((1,), (1,)), ((), ())), preferred_element_type=F32)
        p = _softmax_rows(jnp.where(own_head, s, jnp.finfo(F32).min))
        o2 = _dot(p, v2)
        outs.append(jnp.concatenate([o2[h * tq:(h + 1) * tq] for h in range(HEADS)], axis=-1))
    o = jnp.concatenate(outs, axis=0)
    y = _dot(o.astype(BF16), wo_ref[...])
    o_ref[...] = x + _rms(y, g_ref[G_XA + 1:G_XA + 2, :])


def _xattn_sample(x, l, norm_g, cache_k, cache_v, wq16, wo16, tq, bb):
    n_batch = cache_k.shape[1]
    x_spec = pl.BlockSpec((bb * tq, D), lambda i: (i, 0))
    kv_spec = pl.BlockSpec((None, bb, N_MEM, HEADS, HEAD_DIM), lambda i: (l, i, 0, 0, 0))
    return pl.pallas_call(
        functools.partial(_xattn_sample_body, bb=bb, tq=tq),
        grid=(n_batch // bb,),
        in_specs=[x_spec, kv_spec, kv_spec, _layer(norm_g.shape[1:], l), _layer(wq16.shape[1:], l),
                  _layer(wo16.shape[1:], l)],
        out_specs=x_spec,
        out_shape=jax.ShapeDtypeStruct(x.shape, F32),
        compiler_params=_params(("arbitrary",)),
        name="xattn_sample",
    )(x, cache_k, cache_v, norm_g, wq16, wo16)


MAX_FIR_STEPS = 2 * SUBLANES


def _causal_conv(buf_ref, w_ref, bias_row, out_ref, n_taps, tt, nb, unrolled=False):
    groups = nb // SUBLANES
    assert groups & (groups - 1) == 0
    FIR_STEPS = MAX_FIR_STEPS if tt % MAX_FIR_STEPS == 0 else SUBLANES
    assert tt % FIR_STEPS == 0
    n_in = FIR_STEPS + n_taps - 1
    n_steps = (tt // FIR_STEPS) * groups
    lane_blocks = [slice(c * LANES, (c + 1) * LANES) for c in range(D // LANES)]

    def taps(cs):
        return [jnp.broadcast_to(w_ref[k:k + 1, cs], (SUBLANES, LANES)) for k in range(n_taps)]

    def fir_step(s, cs, wk, bias):
        if isinstance(s, int):
            base = (s // groups) * (FIR_STEPS * nb) + (s % groups) * SUBLANES
        else:
            t_chunk = lax.shift_right_logical(s, groups.bit_length() - 1)
            base = t_chunk * (FIR_STEPS * nb) + (s & (groups - 1)) * SUBLANES
            base = pl.multiple_of(base, SUBLANES)
        acc = [bias] * FIR_STEPS
        for j in range(n_in):
            xj = buf_ref[pl.ds(base + j * nb, SUBLANES), cs]
            for o in range(FIR_STEPS):
                if 0 <= j - o < n_taps:
                    acc[o] = acc[o] + wk[j - o] * xj
        for o in range(FIR_STEPS):
            out_ref[pl.ds(base + o * nb, SUBLANES), cs] = acc[o]

    for cs in lane_blocks:
        wk = taps(cs)
        bias = jnp.broadcast_to(bias_row[0:1, cs], (SUBLANES, LANES))
        if unrolled:
            for s in range(n_steps):
                fir_step(s, cs, wk, bias)
        else:
            lax.fori_loop(0, n_steps, lambda s, carry: (fir_step(s, cs, wk, bias), carry)[1], 0)


def _mixer_body(*refs, tt, nb, has_state):
    if has_state:
        x_ref, ha_ref, hb_ref, h0_ref = refs[:4]
        refs = refs[4:]
    else:
        x_ref = refs[0]
        refs = refs[1:]
    (vec_ref, win_ref, caw_ref, wa_ref, cbw_ref, lruw_ref, wb_ref, wout_ref,
     o_ref, na_ref, nbuf_ref, nh_ref,
     abuf, bbuf, hbuf, conv_s) = refs

    def vec(r):
        return vec_ref[r:r + 1, :]

    rows = tt * nb
    ha_rows = (CONV_A - 1) * nb
    hb_rows = (CONV_B - 1) * nb
    ti = pl.program_id(1)

    @pl.when(ti == 0)
    def _():
        if has_state:
            abuf[0:ha_rows, :] = ha_ref[...].reshape(ha_rows, D)
            bbuf[0:hb_rows, :] = hb_ref[...].reshape(hb_rows, D)
            hbuf[...] = h0_ref[...]
        else:
            abuf[0:ha_rows, :] = jnp.zeros((ha_rows, D), F32)
            bbuf[0:hb_rows, :] = jnp.zeros((hb_rows, D), F32)
            hbuf[...] = jnp.zeros((nb, D), F32)

    if has_state:
        x = _to_time_major(x_ref[...].reshape(rows, D), nb, tt)
    else:
        x = x_ref[...]
    u = _rms(x, vec(G_MIX)).astype(BF16)

    def proj(s):
        return _dot_cols(u, win_ref, s * D, D) + vec(V_BIN + s)

    abuf[ha_rows:ha_rows + rows, :] = proj(0) * _sigmoid(proj(1))
    _causal_conv(abuf, caw_ref, vec(V_CONV_A_B), conv_s, CONV_A, tt, nb)
    abuf[0:ha_rows, :] = abuf[rows:rows + ha_rows, :]
    ac = conv_s[...]
    mu = jnp.mean(ac, axis=-1, keepdims=True)
    acc = ac - mu
    var = jnp.mean(acc * acc, axis=-1, keepdims=True)
    ln = acc * lax.rsqrt(var + EPS) * vec(V_LN_G) + vec(V_LN_B)
    y_a = _dot_cols((ln * _sigmoid(ln)).astype(BF16), wa_ref, 0, D)

    bbuf[hb_rows:hb_rows + rows, :] = proj(2)
    _causal_conv(bbuf, cbw_ref, vec(V_CONV_B_B), conv_s, CONV_B, tt, nb, unrolled=True)
    bbuf[0:hb_rows, :] = bbuf[rows:rows + hb_rows, :]
    xb = conv_s[...]
    xb16 = xb.astype(BF16)
    ri = [_dot(xb16[:, n * LRU_BLOCK:(n + 1) * LRU_BLOCK], lruw_ref[n]) for n in range(LRU_BLOCKS)]
    r_pre = jnp.concatenate([p[:, :LRU_BLOCK] for p in ri], axis=-1)
    i_pre = jnp.concatenate([p[:, LRU_BLOCK:] for p in ri], axis=-1)
    r = _sigmoid(r_pre + vec(V_LRU_B_A))
    gi = _sigmoid(i_pre + vec(V_LRU_B_X))
    nlam = -vec(V_LAMBDA)
    softplus = jnp.maximum(nlam, 0.0) + jnp.log1p(jnp.exp(-jnp.abs(nlam)))
    log_a = (-LRU_C) * r * softplus
    a = jnp.exp(log_a)
    one_m_a2 = -jnp.tanh(log_a) * (a * a + 1.0)
    drive = jnp.sqrt(one_m_a2) * (gi * xb)
    h_last = hbuf[...]
    h_rows = []
    for t in range(tt):
        h_last = a[t * nb:(t + 1) * nb] * h_last + drive[t * nb:(t + 1) * nb]
        h_rows.append(h_last)
    hbuf[...] = h_last
    hs = jnp.concatenate(h_rows, axis=0)
    bg = proj(3)
    gelu = 0.5 * bg * (1.0 + jnp.tanh(math.sqrt(2.0 / math.pi) * (bg + 0.044715 * (bg * bg * bg))))
    y_b = _dot_cols((hs * gelu).astype(BF16), wb_ref, 0, D)

    merged = _sigmoid(proj(4)) * y_a + _sigmoid(proj(5)) * y_b
    m = _dot_cols(merged.astype(BF16), wout_ref, 0, D)
    y = x + _rms(m, vec(G_MIX + 1))
    if has_state:
        o_ref[...] = _to_batch_major(y, nb, tt).reshape(nb, tt, D)
    else:
        o_ref[...] = y

    @pl.when(ti == pl.num_programs(1) - 1)
    def _():
        na_ref[...] = abuf[0:ha_rows, :].reshape(CONV_A - 1, nb, D)
        nbuf_ref[...] = bbuf[0:hb_rows, :].reshape(CONV_B - 1, nb, D)
        nh_ref[...] = h_last


def _mixer(x, state, l, w, tt, nb):
    has_state = state is not None
    rows = tt * nb
    if has_state:
        n_b = x.shape[0]
        grid = (n_b // nb, 1)
        x_spec = pl.BlockSpec((nb, tt, D), lambda j, i: (j, 0, 0))
    else:
        n_b = nb
        grid = (1, x.shape[0] // rows)
        x_spec = pl.BlockSpec((rows, D), lambda j, i: (i, 0))

    def state_spec(n):
        return pl.BlockSpec((n, nb, D), lambda j, i: (0, j, 0), pipeline_mode=pl.Buffered(1))

    h_spec = pl.BlockSpec((nb, D), lambda j, i: (j, 0))
    weights = [w["vecs"], w["w_in"], w["conv_a_w"], w["w_a_out"], w["conv_b_w"], w["lru_w"],
               w["w_b_out"], w["w_out"]]
    state_in = list(state) if has_state else []
    state_specs = [state_spec(CONV_A - 1), state_spec(CONV_B - 1), h_spec]
    state_in_specs = [pl.BlockSpec((None, CONV_A - 1, nb, D), lambda j, i: (l, 0, j, 0),
                                   pipeline_mode=pl.Buffered(1)),
                      pl.BlockSpec((None, CONV_B - 1, nb, D), lambda j, i: (l, 0, j, 0)),
                      pl.BlockSpec((None, nb, D), lambda j, i: (l, j, 0))]
    return pl.pallas_call(
        functools.partial(_mixer_body, tt=tt, nb=nb, has_state=has_state),
        grid=grid,
        in_specs=[x_spec] + (state_in_specs if has_state else [])
                 + [_layer(a.shape[1:], l) for a in weights],
        out_specs=[x_spec] + state_specs,
        out_shape=[jax.ShapeDtypeStruct(x.shape, F32),
                   jax.ShapeDtypeStruct((CONV_A - 1, n_b, D), F32),
                   jax.ShapeDtypeStruct((CONV_B - 1, n_b, D), F32),
                   jax.ShapeDtypeStruct((n_b, D), F32)],
        scratch_shapes=[pltpu.VMEM(((CONV_A - 1) * nb + rows, D), F32),
                        pltpu.VMEM(((CONV_B - 1) * nb + rows, D), F32),
                        pltpu.VMEM((nb, D), F32),
                        pltpu.VMEM((rows, D), F32)],
        compiler_params=_params(("arbitrary", "arbitrary")),
        name="mixer",
    )(x, *state_in, *weights)


def kernel(x_prompt, x_sample, mem_prompt, cache_mem_k, cache_mem_v, state_conv_a, state_conv_b, state_lru_h, w_in, b_in, conv_a_w, conv_a_b, conv_ln_g, conv_ln_b, w_a_out, conv_b_w, conv_b_b, lru_w_a, lru_b_a, lru_w_x, lru_b_x, lru_lambda, w_b_out, w_out, xa_w_q, xa_w_k, xa_w_v, xa_w_o, ffn1_w_in, ffn1_w_out, ffn2_w_in, ffn2_w_out, norm_g):
    n_b, n_t, _ = x_prompt.shape
    s_b, s_t, _ = x_sample.shape

    def vec(a):
        return a.reshape(DEPTH, 1, a.shape[-1])

    vec_rows = [norm_g, vec(conv_a_b), vec(conv_ln_g), vec(conv_ln_b), vec(conv_b_b), vec(lru_b_a),
                vec(lru_b_x), vec(lru_lambda), b_in.reshape(DEPTH, 6, D)]
    n_rows = sum(a.shape[1] for a in vec_rows)
    vec_rows.append(jnp.zeros((DEPTH, N_VEC_ROWS - n_rows, D), F32))
    mixer_w = dict(
        vecs=jnp.concatenate(vec_rows, axis=1), w_in=w_in.astype(BF16), conv_a_w=conv_a_w,
        w_a_out=w_a_out.astype(BF16), conv_b_w=conv_b_w,
        lru_w=jnp.concatenate([lru_w_a, lru_w_x], axis=-1).astype(BF16),
        w_b_out=w_b_out.astype(BF16), w_out=w_out.astype(BF16))
    ffn_w = {G_FFN1: (ffn1_w_in.astype(BF16), ffn1_w_out.astype(BF16)),
             G_FFN2: (ffn2_w_in.astype(BF16), ffn2_w_out.astype(BF16))}
    wq16 = xa_w_q.reshape(DEPTH, D, D).astype(BF16)
    wo16 = xa_w_o.reshape(DEPTH, D, D).astype(BF16)

    mem_k, mem_v, k16, v16 = _memkv(mem_prompt.reshape(n_b * N_MEM, D), norm_g,
                                    xa_w_k.reshape(DEPTH, D, D).astype(BF16),
                                    xa_w_v.reshape(DEPTH, D, D).astype(BF16))

    tm = PROMPT_ROW_TILE
    p_a, p_b, p_h = [], [], []
    x = x_prompt
    for l in range(DEPTH):
        x = _ffn(x, l, G_FFN1, norm_g, *ffn_w[G_FFN1], tm=tm, relayout="in" if l == 0 else None)
        x, na, nbuf, nh = _mixer(x, None, l, mixer_w, tt=tm // n_b, nb=n_b)
        p_a.append(na)
        p_b.append(nbuf)
        p_h.append(nh)
        x = _xattn_prompt(x, l, norm_g, k16, v16, wq16, wo16, nb=n_b, tt=XATTN_PROMPT_STEPS)
        x = _ffn(x, l, G_FFN2, norm_g, *ffn_w[G_FFN2], tm=tm,
                 relayout="out" if l == DEPTH - 1 else None, n_batch=n_b)
    y_prompt = x

    ts = SAMPLE_ROW_TILE
    s_a, s_bb, s_h = [], [], []
    state_tm = (state_conv_a.transpose(0, 2, 1, 3), state_conv_b.transpose(0, 2, 1, 3), state_lru_h)
    x = x_sample.reshape(s_b * s_t, D)
    for l in range(DEPTH):
        x = _ffn(x, l, G_FFN1, norm_g, *ffn_w[G_FFN1], tm=ts)
        x, na, nbuf, nh = _mixer(x.reshape(s_b, s_t, D), state_tm, l, mixer_w, tt=s_t,
                                 nb=MIXER_SAMPLE_BATCH)
        s_a.append(na)
        s_bb.append(nbuf)
        s_h.append(nh)
        x = _xattn_sample(x.reshape(s_b * s_t, D), l, norm_g, cache_mem_k, cache_mem_v, wq16, wo16,
                          tq=s_t, bb=XATTN_SAMPLE_BATCH)
        x = _ffn(x, l, G_FFN2, norm_g, *ffn_w[G_FFN2], tm=ts)
    y_sample = x.reshape(s_b, s_t, D)

    def batch_major(states):
        return jnp.stack(states).transpose(0, 2, 1, 3)

    return (y_prompt, y_sample, mem_k, mem_v,
            batch_major(p_a), batch_major(p_b), jnp.stack(p_h),
            batch_major(s_a), batch_major(s_bb), jnp.stack(s_h))
```

```python
import functools
import math

import jax
import jax.numpy as jnp
from jax import lax
from jax.experimental import pallas as pl
from jax.experimental.pallas import tpu as pltpu

D = 1024
DEPTH = 2
N_MEM = 256
HEADS = 4
HEAD_DIM = D // HEADS
CONV_A = 31
CONV_B = 4
LRU_BLOCKS = 8
LRU_BLOCK = D // LRU_BLOCKS
LRU_C = 8.0
D_FF = 2816
EPS = 1e-6
G_FFN1, G_MIX, G_XA, G_FFN2 = 0, 2, 4, 7
V_CONV_A_B, V_LN_G, V_LN_B, V_CONV_B_B, V_LRU_B_A, V_LRU_B_X, V_LAMBDA, V_BIN = 9, 10, 11, 12, 13, 14, 15, 16
N_VEC_ROWS = 24

LANES = 128
SUBLANES = 8
MXU_N = 256
FF_CHUNK = MXU_N
N_FF_CHUNKS = D_FF // FF_CHUNK
VMEM_LIMIT = 56 * 1024 * 1024

PROMPT_ROW_TILE = 512
XATTN_PROMPT_STEPS = 128
SAMPLE_ROW_TILE = 512
MIXER_SAMPLE_BATCH = 32
XATTN_SAMPLE_BATCH = 8

F32 = jnp.float32
BF16 = jnp.bfloat16


def _dot(a, b):
    return jnp.dot(a, b, preferred_element_type=F32)


def _dot_cols(a, w_ref, col0, n_cols):
    return jnp.concatenate(
        [_dot(a, w_ref[:, col0 + j:col0 + j + MXU_N]) for j in range(0, n_cols, MXU_N)], axis=-1)


def _rms(x, g):
    ms = jnp.mean(x * x, axis=-1, keepdims=True)
    return x * lax.rsqrt(ms + EPS) * g


def _sigmoid(x):
    return 0.5 + 0.5 * jnp.tanh(0.5 * x)


def _layer(shape, l):
    nd = len(shape)
    return pl.BlockSpec((None,) + tuple(shape), lambda *_: (l,) + (0,) * nd,
                        pipeline_mode=pl.Buffered(1))


def _params(sem):
    return pltpu.CompilerParams(dimension_semantics=sem, vmem_limit_bytes=VMEM_LIMIT)


def _to_time_major(v, nb, tt):
    return jnp.swapaxes(v.reshape(nb, tt, D), 0, 1).reshape(tt * nb, D)


def _to_batch_major(v, nb, tt):
    return jnp.swapaxes(v.reshape(tt, nb, D), 0, 1).reshape(nb * tt, D)


def _ffn_body(x_ref, g_ref, win_ref, wout_ref, o_ref, *, g_row, relayout):
    if relayout == "in":
        nb, tq, _ = x_ref.shape
        x = x_ref[...].reshape(nb * tq, D)
    else:
        x = x_ref[...]
    xn = _rms(x, g_ref[g_row:g_row + 1, :])
    acc = None
    for j in range(N_FF_CHUNKS):
        lo = j * FF_CHUNK
        gate = _dot(xn, win_ref[:, lo:lo + FF_CHUNK])
        up = _dot(xn, win_ref[:, D_FF + lo:D_FF + lo + FF_CHUNK])
        h = gate * _sigmoid(gate) * up
        d = _dot(h, wout_ref[lo:lo + FF_CHUNK, :])
        acc = d if acc is None else acc + d
    y = x + 0.5 * _rms(acc, g_ref[g_row + 1:g_row + 2, :])
    if relayout == "in":
        o_ref[...] = _to_time_major(y, nb, tq)
    elif relayout == "out":
        nb, tq, _ = o_ref.shape
        o_ref[...] = _to_batch_major(y, nb, tq).reshape(nb, tq, D)
    else:
        o_ref[...] = y


def _ffn(x, l, g_row, norm_g, w_in16, w_out16, tm, relayout=None, n_batch=None):
    if relayout == "in":
        n_b, n_t, _ = x.shape
        tq = tm // n_b
        grid = (n_t // tq,)
        x_spec = pl.BlockSpec((n_b, tq, D), lambda i: (0, i, 0))
        o_spec = pl.BlockSpec((tm, D), lambda i: (i, 0))
        out_shape = (n_t * n_b, D)
    elif relayout == "out":
        rows = x.shape[0]
        tq = tm // n_batch
        grid = (rows // tm,)
        x_spec = pl.BlockSpec((tm, D), lambda i: (i, 0))
        o_spec = pl.BlockSpec((n_batch, tq, D), lambda i: (0, i, 0))
        out_shape = (n_batch, rows // n_batch, D)
    else:
        rows = x.shape[0]
        grid = (rows // tm,)
        x_spec = o_spec = pl.BlockSpec((tm, D), lambda i: (i, 0))
        out_shape = (rows, D)
    return pl.pallas_call(
        functools.partial(_ffn_body, g_row=g_row, relayout=relayout),
        grid=grid,
        in_specs=[x_spec, _layer(norm_g.shape[1:], l), _layer(w_in16.shape[1:], l),
                  _layer(w_out16.shape[1:], l)],
        out_specs=o_spec,
        out_shape=jax.ShapeDtypeStruct(out_shape, F32),
        compiler_params=_params(("arbitrary",)),
        name="ffn",
    )(x, norm_g, w_in16, w_out16)


def _memkv_body(m_ref, g_ref, wk_ref, wv_ref, k_ref, v_ref, k16_ref, v16_ref):
    mn = _rms(m_ref[...], g_ref[6:7, :]).astype(BF16)
    k = _dot(mn, wk_ref[...])
    v = _dot(mn, wv_ref[...])
    k_ref[...] = k.reshape(k_ref.shape)
    v_ref[...] = v.reshape(v_ref.shape)
    k16_ref[...] = k.astype(BF16)
    v16_ref[...] = v.astype(BF16)


def _memkv(mem2d, norm_g, wk16, wv16, tm=512):
    rows = mem2d.shape[0]
    w_spec = pl.BlockSpec((None, D, D), lambda l, i: (l, 0, 0))
    o_spec = pl.BlockSpec((None, tm, D), lambda l, i: (l, i, 0))
    o5_spec = pl.BlockSpec((None, tm // N_MEM, N_MEM, HEADS, HEAD_DIM), lambda l, i: (l, i, 0, 0, 0))
    return pl.pallas_call(
        _memkv_body,
        grid=(DEPTH, rows // tm),
        in_specs=[pl.BlockSpec((tm, D), lambda l, i: (i, 0)),
                  pl.BlockSpec((None,) + norm_g.shape[1:], lambda l, i: (l, 0, 0)),
                  w_spec, w_spec],
        out_specs=[o5_spec] * 2 + [o_spec] * 2,
        out_shape=[jax.ShapeDtypeStruct((DEPTH, rows // N_MEM, N_MEM, HEADS, HEAD_DIM), F32)] * 2
                  + [jax.ShapeDtypeStruct((DEPTH, rows, D), BF16)] * 2,
        compiler_params=_params(("arbitrary", "arbitrary")),
        name="memkv",
    )(mem2d, norm_g, wk16, wv16)


def _softmax_rows(s):
    e = jnp.exp(s - jnp.max(s, axis=-1, keepdims=True))
    return e * (1.0 / jnp.sum(e, axis=-1, keepdims=True))


def _xattn_prompt_body(x_ref, k_ref, v_ref, g_ref, wq_ref, wo_ref, o_ref, q_s, o_s, *, nb):
    rows = x_ref.shape[0]
    tt = rows // nb
    n_lb = D // LANES
    lb_per_head = HEAD_DIM // LANES
    x = x_ref[...]
    u = _rms(x, g_ref[G_XA:G_XA + 1, :]).astype(BF16)
    q = _dot(u, wq_ref[...]) * (HEAD_DIM ** -0.5)
    for c in range(n_lb):
        q_s[c] = q[:, c * LANES:(c + 1) * LANES]
    scores = []
    for b in range(nb):
        for h in range(HEADS):
            qbh = jnp.concatenate(
                [q_s[h * lb_per_head + i, pl.ds(b, tt, stride=nb), :] for i in range(lb_per_head)],
                axis=-1).astype(BF16)
            scores.append(lax.dot_general(
                qbh, k_ref[b * N_MEM:(b + 1) * N_MEM, h * HEAD_DIM:(h + 1) * HEAD_DIM],
                (((1,), (1,)), ((), ())), preferred_element_type=F32))
    p = _softmax_rows(jnp.stack(scores)).astype(BF16)
    for b in range(nb):
        for h in range(HEADS):
            obh = _dot(p[b * HEADS + h],
                       v_ref[b * N_MEM:(b + 1) * N_MEM, h * HEAD_DIM:(h + 1) * HEAD_DIM])
            for i in range(lb_per_head):
                o_s[h * lb_per_head + i, pl.ds(b, tt, stride=nb), :] = obh[:, i * LANES:(i + 1) * LANES]
    o = jnp.concatenate([o_s[c] for c in range(n_lb)], axis=-1)
    y = _dot(o.astype(BF16), wo_ref[...])
    o_ref[...] = x + _rms(y, g_ref[G_XA + 1:G_XA + 2, :])


def _xattn_prompt(x, l, norm_g, k16, v16, wq16, wo16, nb, tt):
    rows = tt * nb
    x_spec = pl.BlockSpec((rows, D), lambda i: (i, 0))
    return pl.pallas_call(
        functools.partial(_xattn_prompt_body, nb=nb),
        grid=(x.shape[0] // rows,),
        in_specs=[x_spec, _layer(k16.shape[1:], l), _layer(v16.shape[1:], l),
                  _layer(norm_g.shape[1:], l), _layer(wq16.shape[1:], l), _layer(wo16.shape[1:], l)],
        out_specs=x_spec,
        out_shape=jax.ShapeDtypeStruct(x.shape, F32),
        scratch_shapes=[pltpu.VMEM((D // LANES, rows, LANES), F32)] * 2,
        compiler_params=_params(("arbitrary",)),
        name="xattn_prompt",
    )(x, k16, v16, norm_g, wq16, wo16)


def _xattn_sample_body(x_ref, k_ref, v_ref, g_ref, wq_ref, wo_ref, o_ref, *, bb, tq):
    x = x_ref[...]
    u = _rms(x, g_ref[G_XA:G_XA + 1, :]).astype(BF16)
    q = _dot(u, wq_ref[...]) * (HEAD_DIM ** -0.5)
    n_q, n_k = HEADS * tq, N_MEM * HEADS
    row_head = lax.broadcasted_iota(jnp.int32, (n_q, n_k), 0) // tq
    col_head = lax.broadcasted_iota(jnp.int32, (n_q, n_k), 1) % HEADS
    own_head = row_head == col_head
    outs = []
    for b in range(bb):
        qb = q[b * tq:(b + 1) * tq]
        q2 = jnp.concatenate([qb[:, h * HEAD_DIM:(h + 1) * HEAD_DIM] for h in range(HEADS)], axis=0)
        k2 = k_ref[b].reshape(n_k, HEAD_DIM)
        v2 = v_ref[b].reshape(n_k, HEAD_DIM)
        s = lax.dot_general(q2, k2, (((1,), (1,)), ((), ())), preferred_element_type=F32)
        p = _softmax_rows(jnp.where(own_head, s, jnp.finfo(F32).min))
        o2 = _dot(p, v2)
        outs.append(jnp.concatenate([o2[h * tq:(h + 1) * tq] for h in range(HEADS)], axis=-1))
    o = jnp.concatenate(outs, axis=0)
    y = _dot(o.astype(BF16), wo_ref[...])
    o_ref[...] = x + _rms(y, g_ref[G_XA + 1:G_XA + 2, :])


def _xattn_sample(x, l, norm_g, cache_k, cache_v, wq16, wo16, tq, bb):
    n_batch = cache_k.shape[1]
    x_spec = pl.BlockSpec((bb * tq, D), lambda i: (i, 0))
    kv_spec = pl.BlockSpec((None, bb, N_MEM, HEADS, HEAD_DIM), lambda i: (l, i, 0, 0, 0))
    return pl.pallas_call(
        functools.partial(_xattn_sample_body, bb=bb, tq=tq),
        grid=(n_batch // bb,),
        in_specs=[x_spec, kv_spec, kv_spec, _layer(norm_g.shape[1:], l), _layer(wq16.shape[1:], l),
                  _layer(wo16.shape[1:], l)],
        out_specs=x_spec,
        out_shape=jax.ShapeDtypeStruct(x.shape, F32),
        compiler_params=_params(("arbitrary",)),
        name="xattn_sample",
    )(x, cache_k, cache_v, norm_g, wq16, wo16)


MAX_FIR_STEPS = 2 * SUBLANES


def _causal_conv(buf_ref, w_ref, bias_row, out_ref, n_taps, tt, nb, unrolled=False):
    groups = nb // SUBLANES
    assert groups & (groups - 1) == 0
    FIR_STEPS = MAX_FIR_STEPS if tt % MAX_FIR_STEPS == 0 else SUBLANES
    assert tt % FIR_STEPS == 0
    n_in = FIR_STEPS + n_taps - 1
    n_steps = (tt // FIR_STEPS) * groups
    lane_blocks = [slice(c * LANES, (c + 1) * LANES) for c in range(D // LANES)]

    def taps(cs):
        return [jnp.broadcast_to(w_ref[k:k + 1, cs], (SUBLANES, LANES)) for k in range(n_taps)]

    def fir_step(s, cs, wk, bias):
        if isinstance(s, int):
            base = (s // groups) * (FIR_STEPS * nb) + (s % groups) * SUBLANES
        else:
            t_chunk = lax.shift_right_logical(s, groups.bit_length() - 1)
            base = t_chunk * (FIR_STEPS * nb) + (s & (groups - 1)) * SUBLANES
            base = pl.multiple_of(base, SUBLANES)
        acc = [bias] * FIR_STEPS
        for j in range(n_in):
            xj = buf_ref[pl.ds(base + j * nb, SUBLANES), cs]
            for o in range(FIR_STEPS):
                if 0 <= j - o < n_taps:
                    acc[o] = acc[o] + wk[j - o] * xj
        for o in range(FIR_STEPS):
            out_ref[pl.ds(base + o * nb, SUBLANES), cs] = acc[o]

    for cs in lane_blocks:
        wk = taps(cs)
        bias = jnp.broadcast_to(bias_row[0:1, cs], (SUBLANES, LANES))
        if unrolled:
            for s in range(n_steps):
                fir_step(s, cs, wk, bias)
        else:
            lax.fori_loop(0, n_steps, lambda s, carry: (fir_step(s, cs, wk, bias), carry)[1], 0)


def _mixer_body(*refs, tt, nb, has_state):
    if has_state:
        x_ref, ha_ref, hb_ref, h0_ref = refs[:4]
        refs = refs[4:]
    else:
        x_ref = refs[0]
        refs = refs[1:]
    (vec_ref, win_ref, caw_ref, wa_ref, cbw_ref, lruw_ref, wb_ref, wout_ref,
     o_ref, na_ref, nbuf_ref, nh_ref,
     abuf, bbuf, hbuf, conv_s) = refs

    def vec(r):
        return vec_ref[r:r + 1, :]

    rows = tt * nb
    ha_rows = (CONV_A - 1) * nb
    hb_rows = (CONV_B - 1) * nb
    ti = pl.program_id(1)

    @pl.when(ti == 0)
    def _():
        if has_state:
            abuf[0:ha_rows, :] = ha_ref[...].reshape(ha_rows, D)
            bbuf[0:hb_rows, :] = hb_ref[...].reshape(hb_rows, D)
            hbuf[...] = h0_ref[...]
        else:
            abuf[0:ha_rows, :] = jnp.zeros((ha_rows, D), F32)
            bbuf[0:hb_rows, :] = jnp.zeros((hb_rows, D), F32)
            hbuf[...] = jnp.zeros((nb, D), F32)

    if has_state:
        x = _to_time_major(x_ref[...].reshape(rows, D), nb, tt)
    else:
        x = x_ref[...]
    u = _rms(x, vec(G_MIX)).astype(BF16)

    def proj(s):
        return _dot_cols(u, win_ref, s * D, D) + vec(V_BIN + s)

    abuf[ha_rows:ha_rows + rows, :] = proj(0) * _sigmoid(proj(1))
    _causal_conv(abuf, caw_ref, vec(V_CONV_A_B), conv_s, CONV_A, tt, nb)
    abuf[0:ha_rows, :] = abuf[rows:rows + ha_rows, :]
    ac = conv_s[...]
    mu = jnp.mean(ac, axis=-1, keepdims=True)
    acc = ac - mu
    var = jnp.mean(acc * acc, axis=-1, keepdims=True)
    ln = acc * lax.rsqrt(var + EPS) * vec(V_LN_G) + vec(V_LN_B)
    y_a = _dot_cols((ln * _sigmoid(ln)).astype(BF16), wa_ref, 0, D)

    bbuf[hb_rows:hb_rows + rows, :] = proj(2)
    _causal_conv(bbuf, cbw_ref, vec(V_CONV_B_B), conv_s, CONV_B, tt, nb, unrolled=True)
    bbuf[0:hb_rows, :] = bbuf[rows:rows + hb_rows, :]
    xb = conv_s[...]
    xb16 = xb.astype(BF16)
    ri = [_dot(xb16[:, n * LRU_BLOCK:(n + 1) * LRU_BLOCK], lruw_ref[n]) for n in range(LRU_BLOCKS)]
    r_pre = jnp.concatenate([p[:, :LRU_BLOCK] for p in ri], axis=-1)
    i_pre = jnp.concatenate([p[:, LRU_BLOCK:] for p in ri], axis=-1)
    r = _sigmoid(r_pre + vec(V_LRU_B_A))
    gi = _sigmoid(i_pre + vec(V_LRU_B_X))
    nlam = -vec(V_LAMBDA)
    softplus = jnp.maximum(nlam, 0.0) + jnp.log1p(jnp.exp(-jnp.abs(nlam)))
    log_a = (-LRU_C) * r * softplus
    a = jnp.exp(log_a)
    one_m_a2 = -jnp.tanh(log_a) * (a * a + 1.0)
    drive = jnp.sqrt(one_m_a2) * (gi * xb)
    h_last = hbuf[...]
    h_rows = []
    for t in range(tt):
        h_last = a[t * nb:(t + 1) * nb] * h_last + drive[t * nb:(t + 1) * nb]
        h_rows.append(h_last)
    hbuf[...] = h_last
    hs = jnp.concatenate(h_rows, axis=0)
    bg = proj(3)
    gelu = 0.5 * bg * (1.0 + jnp.tanh(math.sqrt(2.0 / math.pi) * (bg + 0.044715 * (bg * bg * bg))))
    y_b = _dot_cols((hs * gelu).astype(BF16), wb_ref, 0, D)

    merged = _sigmoid(proj(4)) * y_a + _sigmoid(proj(5)) * y_b
    m = _dot_cols(merged.astype(BF16), wout_ref, 0, D)
    y = x + _rms(m, vec(G_MIX + 1))
    if has_state:
        o_ref[...] = _to_batch_major(y, nb, tt).reshape(nb, tt, D)
    else:
        o_ref[...] = y

    @pl.when(ti == pl.num_programs(1) - 1)
    def _():
        na_ref[...] = abuf[0:ha_rows, :].reshape(CONV_A - 1, nb, D)
        nbuf_ref[...] = bbuf[0:hb_rows, :].reshape(CONV_B - 1, nb, D)
        nh_ref[...] = h_last


def _mixer(x, state, l, w, tt, nb):
    has_state = state is not None
    rows = tt * nb
    if has_state:
        n_b = x.shape[0]
        grid = (n_b // nb, 1)
        x_spec = pl.BlockSpec((nb, tt, D), lambda j, i: (j, 0, 0))
    else:
        n_b = nb
        grid = (1, x.shape[0] // rows)
        x_spec = pl.BlockSpec((rows, D), lambda j, i: (i, 0))

    def state_spec(n):
        return pl.BlockSpec((n, nb, D), lambda j, i: (0, j, 0), pipeline_mode=pl.Buffered(1))

    h_spec = pl.BlockSpec((nb, D), lambda j, i: (j, 0))
    weights = [w["vecs"], w["w_in"], w["conv_a_w"], w["w_a_out"], w["conv_b_w"], w["lru_w"],
               w["w_b_out"], w["w_out"]]
    state_in = list(state) if has_state else []
    state_specs = [state_spec(CONV_A - 1), state_spec(CONV_B - 1), h_spec]
    state_in_specs = [pl.BlockSpec((None, CONV_A - 1, nb, D), lambda j, i: (l, 0, j, 0),
                                   pipeline_mode=pl.Buffered(1)),
                      pl.BlockSpec((None, CONV_B - 1, nb, D), lambda j, i: (l, 0, j, 0)),
                      pl.BlockSpec((None, nb, D), lambda j, i: (l, j, 0))]
    return pl.pallas_call(
        functools.partial(_mixer_body, tt=tt, nb=nb, has_state=has_state),
        grid=grid,
        in_specs=[x_spec] + (state_in_specs if has_state else [])
                 + [_layer(a.shape[1:], l) for a in weights],
        out_specs=[x_spec] + state_specs,
        out_shape=[jax.ShapeDtypeStruct(x.shape, F32),
                   jax.ShapeDtypeStruct((CONV_A - 1, n_b, D), F32),
                   jax.ShapeDtypeStruct((CONV_B - 1, n_b, D), F32),
                   jax.ShapeDtypeStruct((n_b, D), F32)],
        scratch_shapes=[pltpu.VMEM(((CONV_A - 1) * nb + rows, D), F32),
                        pltpu.VMEM(((CONV_B - 1) * nb + rows, D), F32),
                        pltpu.VMEM((nb, D), F32),
                        pltpu.VMEM((rows, D), F32)],
        compiler_params=_params(("arbitrary", "arbitrary")),
        name="mixer",
    )(x, *state_in, *weights)


def kernel(x_prompt, x_sample, mem_prompt, cache_mem_k, cache_mem_v, state_conv_a, state_conv_b, state_lru_h, w_in, b_in, conv_a_w, conv_a_b, conv_ln_g, conv_ln_b, w_a_out, conv_b_w, conv_b_b, lru_w_a, lru_b_a, lru_w_x, lru_b_x, lru_lambda, w_b_out, w_out, xa_w_q, xa_w_k, xa_w_v, xa_w_o, ffn1_w_in, ffn1_w_out, ffn2_w_in, ffn2_w_out, norm_g):
    n_b, n_t, _ = x_prompt.shape
    s_b, s_t, _ = x_sample.shape

    def vec(a):
        return a.reshape(DEPTH, 1, a.shape[-1])

    vec_rows = [norm_g, vec(conv_a_b), vec(conv_ln_g), vec(conv_ln_b), vec(conv_b_b), vec(lru_b_a),
                vec(lru_b_x), vec(lru_lambda), b_in.reshape(DEPTH, 6, D)]
    n_rows = sum(a.shape[1] for a in vec_rows)
    vec_rows.append(jnp.zeros((DEPTH, N_VEC_ROWS - n_rows, D), F32))
    mixer_w = dict(
        vecs=jnp.concatenate(vec_rows, axis=1), w_in=w_in.astype(BF16), conv_a_w=conv_a_w,
        w_a_out=w_a_out.astype(BF16), conv_b_w=conv_b_w,
        lru_w=jnp.concatenate([lru_w_a, lru_w_x], axis=-1).astype(BF16),
        w_b_out=w_b_out.astype(BF16), w_out=w_out.astype(BF16))
    ffn_w = {G_FFN1: (ffn1_w_in, ffn1_w_out), G_FFN2: (ffn2_w_in, ffn2_w_out)}
    wq16 = xa_w_q.reshape(DEPTH, D, D).astype(BF16)
    wo16 = xa_w_o.reshape(DEPTH, D, D).astype(BF16)

    mem_k, mem_v, k16, v16 = _memkv(mem_prompt.reshape(n_b * N_MEM, D), norm_g,
                                    xa_w_k.reshape(DEPTH, D, D).astype(BF16),
                                    xa_w_v.reshape(DEPTH, D, D).astype(BF16))

    tm = PROMPT_ROW_TILE
    p_a, p_b, p_h = [], [], []
    x = x_prompt
    for l in range(DEPTH):
        x = _ffn(x, l, G_FFN1, norm_g, *ffn_w[G_FFN1], tm=tm, relayout="in" if l == 0 else None)
        x, na, nbuf, nh = _mixer(x, None, l, mixer_w, tt=tm // n_b, nb=n_b)
        p_a.append(na)
        p_b.append(nbuf)
        p_h.append(nh)
        x = _xattn_prompt(x, l, norm_g, k16, v16, wq16, wo16, nb=n_b, tt=XATTN_PROMPT_STEPS)
        x = _ffn(x, l, G_FFN2, norm_g, *ffn_w[G_FFN2], tm=tm,
                 relayout="out" if l == DEPTH - 1 else None, n_batch=n_b)
    y_prompt = x

    ts = SAMPLE_ROW_TILE
    s_a, s_bb, s_h = [], [], []
    state_tm = (state_conv_a.transpose(0, 2, 1, 3), state_conv_b.transpose(0, 2, 1, 3), state_lru_h)
    x = x_sample.reshape(s_b * s_t, D)
    for l in range(DEPTH):
        x = _ffn(x, l, G_FFN1, norm_g, *ffn_w[G_FFN1], tm=ts)
        x, na, nbuf, nh = _mixer(x.reshape(s_b, s_t, D), state_tm, l, mixer_w, tt=s_t,
                                 nb=MIXER_SAMPLE_BATCH)
        s_a.append(na)
        s_bb.append(nbuf)
        s_h.append(nh)
        x = _xattn_sample(x.reshape(s_b * s_t, D), l, norm_g, cache_mem_k, cache_mem_v, wq16, wo16,
                          tq=s_t, bb=XATTN_SAMPLE_BATCH)
        x = _ffn(x, l, G_FFN2, norm_g, *ffn_w[G_FFN2], tm=ts)
    y_sample = x.reshape(s_b, s_t, D)

    def batch_major(states):
        return jnp.stack(states).transpose(0, 2, 1, 3)

    return (y_prompt, y_sample, mem_k, mem_v,
            batch_major(p_a), batch_major(p_b), jnp.stack(p_h),
            batch_major(s_a), batch_major(s_bb), jnp.stack(s_h))
```

```python
import functools
import math

import jax
import jax.numpy as jnp
from jax import lax
from jax.experimental import pallas as pl
from jax.experimental.pallas import tpu as pltpu

D = 1024
DEPTH = 2
N_MEM = 256
HEADS = 4
HEAD_DIM = D // HEADS
CONV_A = 31
CONV_B = 4
LRU_BLOCKS = 8
LRU_BLOCK = D // LRU_BLOCKS
LRU_C = 8.0
D_FF = 2816
EPS = 1e-6
G_FFN1, G_MIX, G_XA, G_FFN2 = 0, 2, 4, 7
V_CONV_A_B, V_LN_G, V_LN_B, V_CONV_B_B, V_LRU_B_A, V_LRU_B_X, V_LAMBDA, V_BIN = 9, 10, 11, 12, 13, 14, 15, 16
N_VEC_ROWS = 24

LANES = 128
SUBLANES = 8
MXU_N = 256
FF_CHUNK = MXU_N
N_FF_CHUNKS = D_FF // FF_CHUNK
VMEM_LIMIT = 56 * 1024 * 1024

PROMPT_ROW_TILE = 512
XATTN_PROMPT_STEPS = 128
SAMPLE_ROW_TILE = 512
MIXER_SAMPLE_BATCH = 32
XATTN_SAMPLE_BATCH = 8

F32 = jnp.float32
BF16 = jnp.bfloat16


def _dot(a, b):
    return jnp.dot(a, b, preferred_element_type=F32)


def _dot_cols(a, w_ref, col0, n_cols):
    return jnp.concatenate(
        [_dot(a, w_ref[:, col0 + j:col0 + j + MXU_N]) for j in range(0, n_cols, MXU_N)], axis=-1)


def _rms(x, g):
    ms = jnp.mean(x * x, axis=-1, keepdims=True)
    return x * lax.rsqrt(ms + EPS) * g


def _sigmoid(x):
    return 0.5 + 0.5 * jnp.tanh(0.5 * x)


def _layer(shape, l):
    nd = len(shape)
    return pl.BlockSpec((None,) + tuple(shape), lambda *_: (l,) + (0,) * nd,
                        pipeline_mode=pl.Buffered(1))


def _params(sem):
    return pltpu.CompilerParams(dimension_semantics=sem, vmem_limit_bytes=VMEM_LIMIT)


def _to_time_major(v, nb, tt):
    return jnp.swapaxes(v.reshape(nb, tt, D), 0, 1).reshape(tt * nb, D)


def _to_batch_major(v, nb, tt):
    return jnp.swapaxes(v.reshape(tt, nb, D), 0, 1).reshape(nb * tt, D)


def _ffn_body(x_ref, g_ref, win_ref, wout_ref, o_ref, *, g_row, relayout):
    if relayout == "in":
        nb, tq, _ = x_ref.shape
        x = x_ref[...].reshape(nb * tq, D)
    else:
        x = x_ref[...]
    xn = _rms(x, g_ref[g_row:g_row + 1, :])
    acc = None
    for j in range(N_FF_CHUNKS):
        lo = j * FF_CHUNK
        gate = _dot(xn, win_ref[:, lo:lo + FF_CHUNK])
        up = _dot(xn, win_ref[:, D_FF + lo:D_FF + lo + FF_CHUNK])
        h = gate * _sigmoid(gate) * up
        d = _dot(h, wout_ref[lo:lo + FF_CHUNK, :])
        acc = d if acc is None else acc + d
    y = x + 0.5 * _rms(acc, g_ref[g_row + 1:g_row + 2, :])
    if relayout == "in":
        o_ref[...] = _to_time_major(y, nb, tq)
    elif relayout == "out":
        nb, tq, _ = o_ref.shape
        o_ref[...] = _to_batch_major(y, nb, tq).reshape(nb, tq, D)
    else:
        o_ref[...] = y


def _ffn(x, l, g_row, norm_g, ffn_w_in, ffn_w_out, tm, relayout=None, n_batch=None):
    if relayout == "in":
        n_b, n_t, _ = x.shape
        tq = tm // n_b
        grid = (n_t // tq,)
        x_spec = pl.BlockSpec((n_b, tq, D), lambda i: (0, i, 0))
        o_spec = pl.BlockSpec((tm, D), lambda i: (i, 0))
        out_shape = (n_t * n_b, D)
    elif relayout == "out":
        rows = x.shape[0]
        tq = tm // n_batch
        grid = (rows // tm,)
        x_spec = pl.BlockSpec((tm, D), lambda i: (i, 0))
        o_spec = pl.BlockSpec((n_batch, tq, D), lambda i: (0, i, 0))
        out_shape = (n_batch, rows // n_batch, D)
    else:
        rows = x.shape[0]
        grid = (rows // tm,)
        x_spec = o_spec = pl.BlockSpec((tm, D), lambda i: (i, 0))
        out_shape = (rows, D)
    return pl.pallas_call(
        functools.partial(_ffn_body, g_row=g_row, relayout=relayout),
        grid=grid,
        in_specs=[x_spec, _layer(norm_g.shape[1:], l), _layer(ffn_w_in.shape[1:], l),
                  _layer(ffn_w_out.shape[1:], l)],
        out_specs=o_spec,
        out_shape=jax.ShapeDtypeStruct(out_shape, F32),
        compiler_params=_params(("arbitrary",)),
        name="ffn",
    )(x, norm_g, ffn_w_in, ffn_w_out)


def _memkv_body(m_ref, g_ref, wk_ref, wv_ref, k_ref, v_ref, k16_ref, v16_ref):
    mn = _rms(m_ref[...], g_ref[6:7, :])
    k = _dot(mn, wk_ref[...])
    v = _dot(mn, wv_ref[...])
    k_ref[...] = k.reshape(k_ref.shape)
    v_ref[...] = v.reshape(v_ref.shape)
    k16_ref[...] = k.astype(BF16)
    v16_ref[...] = v.astype(BF16)


def _memkv(mem2d, norm_g, wk, wv, tm=512):
    rows = mem2d.shape[0]
    w_spec = pl.BlockSpec((None, D, D), lambda l, i: (l, 0, 0))
    o_spec = pl.BlockSpec((None, tm, D), lambda l, i: (l, i, 0))
    o5_spec = pl.BlockSpec((None, tm // N_MEM, N_MEM, HEADS, HEAD_DIM), lambda l, i: (l, i, 0, 0, 0))
    return pl.pallas_call(
        _memkv_body,
        grid=(DEPTH, rows // tm),
        in_specs=[pl.BlockSpec((tm, D), lambda l, i: (i, 0)),
                  pl.BlockSpec((None,) + norm_g.shape[1:], lambda l, i: (l, 0, 0)),
                  w_spec, w_spec],
        out_specs=[o5_spec] * 2 + [o_spec] * 2,
        out_shape=[jax.ShapeDtypeStruct((DEPTH, rows // N_MEM, N_MEM, HEADS, HEAD_DIM), F32)] * 2
                  + [jax.ShapeDtypeStruct((DEPTH, rows, D), BF16)] * 2,
        compiler_params=_params(("arbitrary", "arbitrary")),
        name="memkv",
    )(mem2d, norm_g, wk, wv)


def _softmax_rows(s):
    e = jnp.exp(s - jnp.max(s, axis=-1, keepdims=True))
    return e * (1.0 / jnp.sum(e, axis=-1, keepdims=True))


def _xattn_prompt_body(x_ref, k_ref, v_ref, g_ref, wq_ref, wo_ref, o_ref, q_s, o_s, *, nb):
    rows = x_ref.shape[0]
    tt = rows // nb
    n_lb = D // LANES
    lb_per_head = HEAD_DIM // LANES
    x = x_ref[...]
    u = _rms(x, g_ref[G_XA:G_XA + 1, :])
    q = _dot(u, wq_ref[...]) * (HEAD_DIM ** -0.5)
    for c in range(n_lb):
        q_s[c] = q[:, c * LANES:(c + 1) * LANES]
    scores = []
    for b in range(nb):
        for h in range(HEADS):
            qbh = jnp.concatenate(
                [q_s[h * lb_per_head + i, pl.ds(b, tt, stride=nb), :] for i in range(lb_per_head)],
                axis=-1).astype(BF16)
            scores.append(lax.dot_general(
                qbh, k_ref[b * N_MEM:(b + 1) * N_MEM, h * HEAD_DIM:(h + 1) * HEAD_DIM],
                (((1,), (1,)), ((), ())), preferred_element_type=F32))
    p = _softmax_rows(jnp.stack(scores)).astype(BF16)
    for b in range(nb):
        for h in range(HEADS):
            obh = _dot(p[b * HEADS + h],
                       v_ref[b * N_MEM:(b + 1) * N_MEM, h * HEAD_DIM:(h + 1) * HEAD_DIM])
            for i in range(lb_per_head):
                o_s[h * lb_per_head + i, pl.ds(b, tt, stride=nb), :] = obh[:, i * LANES:(i + 1) * LANES]
    o = jnp.concatenate([o_s[c] for c in range(n_lb)], axis=-1)
    y = _dot(o, wo_ref[...])
    o_ref[...] = x + _rms(y, g_ref[G_XA + 1:G_XA + 2, :])


def _xattn_prompt(x, l, norm_g, k16, v16, wq, wo, nb, tt):
    rows = tt * nb
    x_spec = pl.BlockSpec((rows, D), lambda i: (i, 0))
    return pl.pallas_call(
        functools.partial(_xattn_prompt_body, nb=nb),
        grid=(x.shape[0] // rows,),
        in_specs=[x_spec, _layer(k16.shape[1:], l), _layer(v16.shape[1:], l),
                  _layer(norm_g.shape[1:], l), _layer(wq.shape[1:], l), _layer(wo.shape[1:], l)],
        out_specs=x_spec,
        out_shape=jax.ShapeDtypeStruct(x.shape, F32),
        scratch_shapes=[pltpu.VMEM((D // LANES, rows, LANES), F32)] * 2,
        compiler_params=_params(("arbitrary",)),
        name="xattn_prompt",
    )(x, k16, v16, norm_g, wq, wo)


def _xattn_sample_body(x_ref, k_ref, v_ref, g_ref, wq_ref, wo_ref, o_ref, *, bb, tq):
    x = x_ref[...]
    u = _rms(x, g_ref[G_XA:G_XA + 1, :])
    q = _dot(u, wq_ref[...]) * (HEAD_DIM ** -0.5)
    n_q, n_k = HEADS * tq, N_MEM * HEADS
    row_head = lax.broadcasted_iota(jnp.int32, (n_q, n_k), 0) // tq
    col_head = lax.broadcasted_iota(jnp.int32, (n_q, n_k), 1) % HEADS
    own_head = row_head == col_head
    outs = []
    for b in range(bb):
        qb = q[b * tq:(b + 1) * tq]
        q2 = jnp.concatenate([qb[:, h * HEAD_DIM:(h + 1) * HEAD_DIM] for h in range(HEADS)], axis=0)
        k2 = k_ref[b].reshape(n_k, HEAD_DIM)
        v2 = v_ref[b].reshape(n_k, HEAD_DIM)
        s = lax.dot_general(q2, k2, (((1,), (1,)), ((), ())), preferred_element_type=F32)
        p = _softmax_rows(jnp.where(own_head, s, jnp.finfo(F32).min))
        o2 = _dot(p, v2)
        outs.append(jnp.concatenate([o2[h * tq:(h + 1) * tq] for h in range(HEADS)], axis=-1))
    o = jnp.concatenate(outs, axis=0)
    y = _dot(o, wo_ref[...])
    o_ref[...] = x + _rms(y, g_ref[G_XA + 1:G_XA + 2, :])


def _xattn_sample(x, l, norm_g, cache_k, cache_v, wq, wo, tq, bb):
    n_batch = cache_k.shape[1]
    x_spec = pl.BlockSpec((bb * tq, D), lambda i: (i, 0))
    kv_spec = pl.BlockSpec((None, bb, N_MEM, HEADS, HEAD_DIM), lambda i: (l, i, 0, 0, 0))
    return pl.pallas_call(
        functools.partial(_xattn_sample_body, bb=bb, tq=tq),
        grid=(n_batch // bb,),
        in_specs=[x_spec, kv_spec, kv_spec, _layer(norm_g.shape[1:], l), _layer(wq.shape[1:], l),
                  _layer(wo.shape[1:], l)],
        out_specs=x_spec,
        out_shape=jax.ShapeDtypeStruct(x.shape, F32),
        compiler_params=_params(("arbitrary",)),
        name="xattn_sample",
    )(x, cache_k, cache_v, norm_g, wq, wo)


MAX_FIR_STEPS = 2 * SUBLANES


def _causal_conv(buf_ref, w_ref, bias_row, out_ref, n_taps, tt, nb, unrolled=False):
    groups = nb // SUBLANES
    assert groups & (groups - 1) == 0
    FIR_STEPS = MAX_FIR_STEPS if tt % MAX_FIR_STEPS == 0 else SUBLANES
    assert tt % FIR_STEPS == 0
    n_in = FIR_STEPS + n_taps - 1
    n_steps = (tt // FIR_STEPS) * groups
    lane_blocks = [slice(c * LANES, (c + 1) * LANES) for c in range(D // LANES)]

    def taps(cs):
        return [jnp.broadcast_to(w_ref[k:k + 1, cs], (SUBLANES, LANES)) for k in range(n_taps)]

    def fir_step(s, cs, wk, bias):
        if isinstance(s, int):
            base = (s // groups) * (FIR_STEPS * nb) + (s % groups) * SUBLANES
        else:
            t_chunk = lax.shift_right_logical(s, groups.bit_length() - 1)
            base = t_chunk * (FIR_STEPS * nb) + (s & (groups - 1)) * SUBLANES
            base = pl.multiple_of(base, SUBLANES)
        acc = [bias] * FIR_STEPS
        for j in range(n_in):
            xj = buf_ref[pl.ds(base + j * nb, SUBLANES), cs]
            for o in range(FIR_STEPS):
                if 0 <= j - o < n_taps:
                    acc[o] = acc[o] + wk[j - o] * xj
        for o in range(FIR_STEPS):
            out_ref[pl.ds(base + o * nb, SUBLANES), cs] = acc[o]

    for cs in lane_blocks:
        wk = taps(cs)
        bias = jnp.broadcast_to(bias_row[0:1, cs], (SUBLANES, LANES))
        if unrolled:
            for s in range(n_steps):
                fir_step(s, cs, wk, bias)
        else:
            lax.fori_loop(0, n_steps, lambda s, carry: (fir_step(s, cs, wk, bias), carry)[1], 0)


def _mixer_body(*refs, tt, nb, has_state):
    if has_state:
        x_ref, ha_ref, hb_ref, h0_ref = refs[:4]
        refs = refs[4:]
    else:
        x_ref = refs[0]
        refs = refs[1:]
    (vec_ref, win_ref, caw_ref, wa_ref, cbw_ref, lruw_ref, wb_ref, wout_ref,
     o_ref, na_ref, nbuf_ref, nh_ref,
     abuf, bbuf, hbuf, conv_s) = refs

    def vec(r):
        return vec_ref[r:r + 1, :]

    rows = tt * nb
    ha_rows = (CONV_A - 1) * nb
    hb_rows = (CONV_B - 1) * nb
    ti = pl.program_id(1)

    @pl.when(ti == 0)
    def _():
        if has_state:
            abuf[0:ha_rows, :] = ha_ref[...].reshape(ha_rows, D)
            bbuf[0:hb_rows, :] = hb_ref[...].reshape(hb_rows, D)
            hbuf[...] = h0_ref[...]
        else:
            abuf[0:ha_rows, :] = jnp.zeros((ha_rows, D), F32)
            bbuf[0:hb_rows, :] = jnp.zeros((hb_rows, D), F32)
            hbuf[...] = jnp.zeros((nb, D), F32)

    if has_state:
        x = _to_time_major(x_ref[...].reshape(rows, D), nb, tt)
    else:
        x = x_ref[...]
    u = _rms(x, vec(G_MIX)).astype(BF16)

    def proj(s):
        return _dot_cols(u, win_ref, s * D, D) + vec(V_BIN + s)

    abuf[ha_rows:ha_rows + rows, :] = proj(0) * _sigmoid(proj(1))
    _causal_conv(abuf, caw_ref, vec(V_CONV_A_B), conv_s, CONV_A, tt, nb)
    abuf[0:ha_rows, :] = abuf[rows:rows + ha_rows, :]
    ac = conv_s[...]
    mu = jnp.mean(ac, axis=-1, keepdims=True)
    acc = ac - mu
    var = jnp.mean(acc * acc, axis=-1, keepdims=True)
    ln = acc * lax.rsqrt(var + EPS) * vec(V_LN_G) + vec(V_LN_B)
    y_a = _dot_cols((ln * _sigmoid(ln)).astype(BF16), wa_ref, 0, D)

    bbuf[hb_rows:hb_rows + rows, :] = proj(2)
    _causal_conv(bbuf, cbw_ref, vec(V_CONV_B_B), conv_s, CONV_B, tt, nb, unrolled=True)
    bbuf[0:hb_rows, :] = bbuf[rows:rows + hb_rows, :]
    xb = conv_s[...]
    xb16 = xb.astype(BF16)
    ri = [_dot(xb16[:, n * LRU_BLOCK:(n + 1) * LRU_BLOCK], lruw_ref[n]) for n in range(LRU_BLOCKS)]
    r_pre = jnp.concatenate([p[:, :LRU_BLOCK] for p in ri], axis=-1)
    i_pre = jnp.concatenate([p[:, LRU_BLOCK:] for p in ri], axis=-1)
    r = _sigmoid(r_pre + vec(V_LRU_B_A))
    gi = _sigmoid(i_pre + vec(V_LRU_B_X))
    nlam = -vec(V_LAMBDA)
    softplus = jnp.maximum(nlam, 0.0) + jnp.log1p(jnp.exp(-jnp.abs(nlam)))
    log_a = (-LRU_C) * r * softplus
    a = jnp.exp(log_a)
    one_m_a2 = -jnp.tanh(log_a) * (a * a + 1.0)
    drive = jnp.sqrt(one_m_a2) * (gi * xb)
    h_last = hbuf[...]
    h_rows = []
    for t in range(tt):
        h_last = a[t * nb:(t + 1) * nb] * h_last + drive[t * nb:(t + 1) * nb]
        h_rows.append(h_last)
    hbuf[...] = h_last
    hs = jnp.concatenate(h_rows, axis=0)
    bg = proj(3)
    gelu = 0.5 * bg * (1.0 + jnp.tanh(math.sqrt(2.0 / math.pi) * (bg + 0.044715 * (bg * bg * bg))))
    y_b = _dot_cols((hs * gelu).astype(BF16), wb_ref, 0, D)

    merged = _sigmoid(proj(4)) * y_a + _sigmoid(proj(5)) * y_b
    m = _dot_cols(merged.astype(BF16), wout_ref, 0, D)
    y = x + _rms(m, vec(G_MIX + 1))
    if has_state:
        o_ref[...] = _to_batch_major(y, nb, tt).reshape(nb, tt, D)
    else:
        o_ref[...] = y

    @pl.when(ti == pl.num_programs(1) - 1)
    def _():
        na_ref[...] = abuf[0:ha_rows, :].reshape(CONV_A - 1, nb, D)
        nbuf_ref[...] = bbuf[0:hb_rows, :].reshape(CONV_B - 1, nb, D)
        nh_ref[...] = h_last


def _mixer(x, state, l, w, tt, nb):
    has_state = state is not None
    rows = tt * nb
    if has_state:
        n_b = x.shape[0]
        grid = (n_b // nb, 1)
        x_spec = pl.BlockSpec((nb, tt, D), lambda j, i: (j, 0, 0))
    else:
        n_b = nb
        grid = (1, x.shape[0] // rows)
        x_spec = pl.BlockSpec((rows, D), lambda j, i: (i, 0))

    def state_spec(n):
        return pl.BlockSpec((n, nb, D), lambda j, i: (0, j, 0), pipeline_mode=pl.Buffered(1))

    h_spec = pl.BlockSpec((nb, D), lambda j, i: (j, 0))
    weights = [w["vecs"], w["w_in"], w["conv_a_w"], w["w_a_out"], w["conv_b_w"], w["lru_w"],
               w["w_b_out"], w["w_out"]]
    state_in = list(state) if has_state else []
    state_specs = [state_spec(CONV_A - 1), state_spec(CONV_B - 1), h_spec]
    state_in_specs = [pl.BlockSpec((None, CONV_A - 1, nb, D), lambda j, i: (l, 0, j, 0),
                                   pipeline_mode=pl.Buffered(1)),
                      pl.BlockSpec((None, CONV_B - 1, nb, D), lambda j, i: (l, 0, j, 0)),
                      pl.BlockSpec((None, nb, D), lambda j, i: (l, j, 0))]
    return pl.pallas_call(
        functools.partial(_mixer_body, tt=tt, nb=nb, has_state=has_state),
        grid=grid,
        in_specs=[x_spec] + (state_in_specs if has_state else [])
                 + [_layer(a.shape[1:], l) for a in weights],
        out_specs=[x_spec] + state_specs,
        out_shape=[jax.ShapeDtypeStruct(x.shape, F32),
                   jax.ShapeDtypeStruct((CONV_A - 1, n_b, D), F32),
                   jax.ShapeDtypeStruct((CONV_B - 1, n_b, D), F32),
                   jax.ShapeDtypeStruct((n_b, D), F32)],
        scratch_shapes=[pltpu.VMEM(((CONV_A - 1) * nb + rows, D), F32),
                        pltpu.VMEM(((CONV_B - 1) * nb + rows, D), F32),
                        pltpu.VMEM((nb, D), F32),
                        pltpu.VMEM((rows, D), F32)],
        compiler_params=_params(("arbitrary", "arbitrary")),
        name="mixer",
    )(x, *state_in, *weights)


def kernel(x_prompt, x_sample, mem_prompt, cache_mem_k, cache_mem_v, state_conv_a, state_conv_b, state_lru_h, w_in, b_in, conv_a_w, conv_a_b, conv_ln_g, conv_ln_b, w_a_out, conv_b_w, conv_b_b, lru_w_a, lru_b_a, lru_w_x, lru_b_x, lru_lambda, w_b_out, w_out, xa_w_q, xa_w_k, xa_w_v, xa_w_o, ffn1_w_in, ffn1_w_out, ffn2_w_in, ffn2_w_out, norm_g):
    n_b, n_t, _ = x_prompt.shape
    s_b, s_t, _ = x_sample.shape

    def vec(a):
        return a.reshape(DEPTH, 1, a.shape[-1])

    vec_rows = [norm_g, vec(conv_a_b), vec(conv_ln_g), vec(conv_ln_b), vec(conv_b_b), vec(lru_b_a),
                vec(lru_b_x), vec(lru_lambda), b_in.reshape(DEPTH, 6, D)]
    n_rows = sum(a.shape[1] for a in vec_rows)
    vec_rows.append(jnp.zeros((DEPTH, N_VEC_ROWS - n_rows, D), F32))
    mixer_w = dict(
        vecs=jnp.concatenate(vec_rows, axis=1), w_in=w_in.astype(BF16), conv_a_w=conv_a_w,
        w_a_out=w_a_out.astype(BF16), conv_b_w=conv_b_w,
        lru_w=jnp.concatenate([lru_w_a, lru_w_x], axis=-1).astype(BF16),
        w_b_out=w_b_out.astype(BF16), w_out=w_out.astype(BF16))
    ffn_w = {G_FFN1: (ffn1_w_in, ffn1_w_out), G_FFN2: (ffn2_w_in, ffn2_w_out)}
    wq = xa_w_q.reshape(DEPTH, D, D)
    wo = xa_w_o.reshape(DEPTH, D, D)

    mem_k, mem_v, k16, v16 = _memkv(mem_prompt.reshape(n_b * N_MEM, D), norm_g,
                                    xa_w_k.reshape(DEPTH, D, D), xa_w_v.reshape(DEPTH, D, D))

    tm = PROMPT_ROW_TILE
    p_a, p_b, p_h = [], [], []
    x = x_prompt
    for l in range(DEPTH):
        x = _ffn(x, l, G_FFN1, norm_g, *ffn_w[G_FFN1], tm=tm, relayout="in" if l == 0 else None)
        x, na, nbuf, nh = _mixer(x, None, l, mixer_w, tt=tm // n_b, nb=n_b)
        p_a.append(na)
        p_b.append(nbuf)
        p_h.append(nh)
        x = _xattn_prompt(x, l, norm_g, k16, v16, wq, wo, nb=n_b, tt=XATTN_PROMPT_STEPS)
        x = _ffn(x, l, G_FFN2, norm_g, *ffn_w[G_FFN2], tm=tm,
                 relayout="out" if l == DEPTH - 1 else None, n_batch=n_b)
    y_prompt = x

    ts = SAMPLE_ROW_TILE
    s_a, s_bb, s_h = [], [], []
    state_tm = (state_conv_a.transpose(0, 2, 1, 3), state_conv_b.transpose(0, 2, 1, 3), state_lru_h)
    x = x_sample.reshape(s_b * s_t, D)
    for l in range(DEPTH):
        x = _ffn(x, l, G_FFN1, norm_g, *ffn_w[G_FFN1], tm=ts)
        x, na, nbuf, nh = _mixer(x.reshape(s_b, s_t, D), state_tm, l, mixer_w, tt=s_t,
                                 nb=MIXER_SAMPLE_BATCH)
        s_a.append(na)
        s_bb.append(nbuf)
        s_h.append(nh)
        x = _xattn_sample(x.reshape(s_b * s_t, D), l, norm_g, cache_mem_k, cache_mem_v, wq, wo,
                          tq=s_t, bb=XATTN_SAMPLE_BATCH)
        x = _ffn(x, l, G_FFN2, norm_g, *ffn_w[G_FFN2], tm=ts)
    y_sample = x.reshape(s_b, s_t, D)

    def batch_major(states):
        return jnp.stack(states).transpose(0, 2, 1, 3)

    return (y_prompt, y_sample, mem_k, mem_v,
            batch_major(p_a), batch_major(p_b), jnp.stack(p_h),
            batch_major(s_a), batch_major(s_bb), jnp.stack(s_h))
```

```python
import functools
import math

import jax
import jax.numpy as jnp
from jax import lax
from jax.experimental import pallas as pl
from jax.experimental.pallas import tpu as pltpu

D = 1024
DEPTH = 2
N_MEM = 256
HEADS = 4
HEAD_DIM = D // HEADS
CONV_A = 31
CONV_B = 4
LRU_BLOCKS = 8
LRU_BLOCK = D // LRU_BLOCKS
LRU_C = 8.0
D_FF = 2816
EPS = 1e-6
G_FFN1, G_MIX, G_XA, G_FFN2 = 0, 2, 4, 7
V_CONV_A_B, V_LN_G, V_LN_B, V_CONV_B_B, V_LRU_B_A, V_LRU_B_X, V_LAMBDA, V_BIN = 9, 10, 11, 12, 13, 14, 15, 16
N_VEC_ROWS = 24

LANES = 128
SUBLANES = 8
MXU_N = 256
FF_CHUNK = MXU_N
N_FF_CHUNKS = D_FF // FF_CHUNK
VMEM_LIMIT = 56 * 1024 * 1024

PROMPT_ROW_TILE = 512
XATTN_PROMPT_STEPS = 128
SAMPLE_ROW_TILE = 512
MIXER_SAMPLE_BATCH = 32
XATTN_SAMPLE_BATCH = 8

F32 = jnp.float32
BF16 = jnp.bfloat16


def _dot(a, b):
    return jnp.dot(a, b, preferred_element_type=F32)


def _dot_cols(a, w_ref, col0, n_cols):
    return jnp.concatenate(
        [_dot(a, w_ref[:, col0 + j:col0 + j + MXU_N]) for j in range(0, n_cols, MXU_N)], axis=-1)


def _rms(x, g):
    ms = jnp.mean(x * x, axis=-1, keepdims=True)
    return x * lax.rsqrt(ms + EPS) * g


def _sigmoid(x):
    return 0.5 + 0.5 * jnp.tanh(0.5 * x)


def _layer(shape, l):
    nd = len(shape)
    return pl.BlockSpec((None,) + tuple(shape), lambda *_: (l,) + (0,) * nd,
                        pipeline_mode=pl.Buffered(1))


def _params(sem):
    return pltpu.CompilerParams(dimension_semantics=sem, vmem_limit_bytes=VMEM_LIMIT)


def _to_time_major(v, nb, tt):
    return jnp.swapaxes(v.reshape(nb, tt, D), 0, 1).reshape(tt * nb, D)


def _to_batch_major(v, nb, tt):
    return jnp.swapaxes(v.reshape(tt, nb, D), 0, 1).reshape(nb * tt, D)


def _ffn_body(x_ref, g_ref, win_ref, wout_ref, o_ref, *, g_row, relayout):
    if relayout == "in":
        nb, tq, _ = x_ref.shape
        x = x_ref[...].reshape(nb * tq, D)
    else:
        x = x_ref[...]
    xn = _rms(x, g_ref[g_row:g_row + 1, :])
    acc = None
    for j in range(N_FF_CHUNKS):
        lo = j * FF_CHUNK
        gate = _dot(xn, win_ref[:, lo:lo + FF_CHUNK])
        up = _dot(xn, win_ref[:, D_FF + lo:D_FF + lo + FF_CHUNK])
        h = gate * _sigmoid(gate) * up
        d = _dot(h, wout_ref[lo:lo + FF_CHUNK, :])
        acc = d if acc is None else acc + d
    y = x + 0.5 * _rms(acc, g_ref[g_row + 1:g_row + 2, :])
    if relayout == "in":
        o_ref[...] = _to_time_major(y, nb, tq)
    elif relayout == "out":
        nb, tq, _ = o_ref.shape
        o_ref[...] = _to_batch_major(y, nb, tq).reshape(nb, tq, D)
    else:
        o_ref[...] = y


def _ffn(x, l, g_row, norm_g, ffn_w_in, ffn_w_out, tm, relayout=None, n_batch=None):
    if relayout == "in":
        n_b, n_t, _ = x.shape
        tq = tm // n_b
        grid = (n_t // tq,)
        x_spec = pl.BlockSpec((n_b, tq, D), lambda i: (0, i, 0))
        o_spec = pl.BlockSpec((tm, D), lambda i: (i, 0))
        out_shape = (n_t * n_b, D)
    elif relayout == "out":
        rows = x.shape[0]
        tq = tm // n_batch
        grid = (rows // tm,)
        x_spec = pl.BlockSpec((tm, D), lambda i: (i, 0))
        o_spec = pl.BlockSpec((n_batch, tq, D), lambda i: (0, i, 0))
        out_shape = (n_batch, rows // n_batch, D)
    else:
        rows = x.shape[0]
        grid = (rows // tm,)
        x_spec = o_spec = pl.BlockSpec((tm, D), lambda i: (i, 0))
        out_shape = (rows, D)
    return pl.pallas_call(
        functools.partial(_ffn_body, g_row=g_row, relayout=relayout),
        grid=grid,
        in_specs=[x_spec, _layer(norm_g.shape[1:], l), _layer(ffn_w_in.shape[1:], l),
                  _layer(ffn_w_out.shape[1:], l)],
        out_specs=o_spec,
        out_shape=jax.ShapeDtypeStruct(out_shape, F32),
        compiler_params=_params(("arbitrary",)),
        name="ffn",
    )(x, norm_g, ffn_w_in, ffn_w_out)


def _memkv_body(m_ref, g_ref, wk_ref, wv_ref, k_ref, v_ref, k16_ref, v16_ref):
    mn = _rms(m_ref[...], g_ref[6:7, :]).astype(BF16)
    k = _dot(mn, wk_ref[...])
    v = _dot(mn, wv_ref[...])
    k_ref[...] = k.reshape(k_ref.shape)
    v_ref[...] = v.reshape(v_ref.shape)
    k16_ref[...] = k.astype(BF16)
    v16_ref[...] = v.astype(BF16)


def _memkv(mem2d, norm_g, wk16, wv16, tm=512):
    rows = mem2d.shape[0]
    w_spec = pl.BlockSpec((None, D, D), lambda l, i: (l, 0, 0))
    o_spec = pl.BlockSpec((None, tm, D), lambda l, i: (l, i, 0))
    o5_spec = pl.BlockSpec((None, tm // N_MEM, N_MEM, HEADS, HEAD_DIM), lambda l, i: (l, i, 0, 0, 0))
    return pl.pallas_call(
        _memkv_body,
        grid=(DEPTH, rows // tm),
        in_specs=[pl.BlockSpec((tm, D), lambda l, i: (i, 0)),
                  pl.BlockSpec((None,) + norm_g.shape[1:], lambda l, i: (l, 0, 0)),
                  w_spec, w_spec],
        out_specs=[o5_spec] * 2 + [o_spec] * 2,
        out_shape=[jax.ShapeDtypeStruct((DEPTH, rows // N_MEM, N_MEM, HEADS, HEAD_DIM), F32)] * 2
                  + [jax.ShapeDtypeStruct((DEPTH, rows, D), BF16)] * 2,
        compiler_params=_params(("arbitrary", "arbitrary")),
        name="memkv",
    )(mem2d, norm_g, wk16, wv16)


def _softmax_rows(s):
    e = jnp.exp(s - jnp.max(s, axis=-1, keepdims=True))
    return e * (1.0 / jnp.sum(e, axis=-1, keepdims=True))


def _xattn_prompt_body(x_ref, k_ref, v_ref, g_ref, wq_ref, wo_ref, o_ref, q_s, o_s, *, nb):
    rows = x_ref.shape[0]
    tt = rows // nb
    n_lb = D // LANES
    lb_per_head = HEAD_DIM // LANES
    x = x_ref[...]
    u = _rms(x, g_ref[G_XA:G_XA + 1, :]).astype(BF16)
    q = _dot(u, wq_ref[...]) * (HEAD_DIM ** -0.5)
    for c in range(n_lb):
        q_s[c] = q[:, c * LANES:(c + 1) * LANES]
    scores = []
    for b in range(nb):
        for h in range(HEADS):
            qbh = jnp.concatenate(
                [q_s[h * lb_per_head + i, pl.ds(b, tt, stride=nb), :] for i in range(lb_per_head)],
                axis=-1).astype(BF16)
            scores.append(lax.dot_general(
                qbh, k_ref[b * N_MEM:(b + 1) * N_MEM, h * HEAD_DIM:(h + 1) * HEAD_DIM],
                (((1,), (1,)), ((), ())), preferred_element_type=F32))
    p = _softmax_rows(jnp.stack(scores)).astype(BF16)
    for b in range(nb):
        for h in range(HEADS):
            obh = _dot(p[b * HEADS + h],
                       v_ref[b * N_MEM:(b + 1) * N_MEM, h * HEAD_DIM:(h + 1) * HEAD_DIM])
            for i in range(lb_per_head):
                o_s[h * lb_per_head + i, pl.ds(b, tt, stride=nb), :] = obh[:, i * LANES:(i + 1) * LANES]
    o = jnp.concatenate([o_s[c] for c in range(n_lb)], axis=-1)
    y = _dot(o.astype(BF16), wo_ref[...])
    o_ref[...] = x + _rms(y, g_ref[G_XA + 1:G_XA + 2, :])


def _xattn_prompt(x, l, norm_g, k16, v16, wq16, wo16, nb, tt):
    rows = tt * nb
    x_spec = pl.BlockSpec((rows, D), lambda i: (i, 0))
    return pl.pallas_call(
        functools.partial(_xattn_prompt_body, nb=nb),
        grid=(x.shape[0] // rows,),
        in_specs=[x_spec, _layer(k16.shape[1:], l), _layer(v16.shape[1:], l),
                  _layer(norm_g.shape[1:], l), _layer(wq16.shape[1:], l), _layer(wo16.shape[1:], l)],
        out_specs=x_spec,
        out_shape=jax.ShapeDtypeStruct(x.shape, F32),
        scratch_shapes=[pltpu.VMEM((D // LANES, rows, LANES), F32)] * 2,
        compiler_params=_params(("arbitrary",)),
        name="xattn_prompt",
    )(x, k16, v16, norm_g, wq16, wo16)


def _xattn_sample_body(x_ref, k_ref, v_ref, g_ref, wq_ref, wo_ref, o_ref, *, bb, tq):
    x = x_ref[...]
    u = _rms(x, g_ref[G_XA:G_XA + 1, :]).astype(BF16)
    q = _dot(u, wq_ref[...]) * (HEAD_DIM ** -0.5)
    n_q, n_k = HEADS * tq, N_MEM * HEADS
    row_head = lax.broadcasted_iota(jnp.int32, (n_q, n_k), 0) // tq
    col_head = lax.broadcasted_iota(jnp.int32, (n_q, n_k), 1) % HEADS
    own_head = row_head == col_head
    outs = []
    for b in range(bb):
        qb = q[b * tq:(b + 1) * tq]
        q2 = jnp.concatenate([qb[:, h * HEAD_DIM:(h + 1) * HEAD_DIM] for h in range(HEADS)], axis=0)
        k2 = k_ref[b].reshape(n_k, HEAD_DIM)
        v2 = v_ref[b].reshape(n_k, HEAD_DIM)
        s = lax.dot_general(q2, k2, (((1,), (1,)), ((), ())), preferred_element_type=F32)
        p = _softmax_rows(jnp.where(own_head, s, jnp.finfo(F32).min))
        o2 = _dot(p, v2)
        outs.append(jnp.concatenate([o2[h * tq:(h + 1) * tq] for h in range(HEADS)], axis=-1))
    o = jnp.concatenate(outs, axis=0)
    y = _dot(o.astype(BF16), wo_ref[...])
    o_ref[...] = x + _rms(y, g_ref[G_XA + 1:G_XA + 2, :])


def _xattn_sample(x, l, norm_g, cache_k, cache_v, wq16, wo16, tq, bb):
    n_batch = cache_k.shape[1]
    x_spec = pl.BlockSpec((bb * tq, D), lambda i: (i, 0))
    kv_spec = pl.BlockSpec((None, bb, N_MEM, HEADS, HEAD_DIM), lambda i: (l, i, 0, 0, 0))
    return pl.pallas_call(
        functools.partial(_xattn_sample_body, bb=bb, tq=tq),
        grid=(n_batch // bb,),
        in_specs=[x_spec, kv_spec, kv_spec, _layer(norm_g.shape[1:], l), _layer(wq16.shape[1:], l),
                  _layer(wo16.shape[1:], l)],
        out_specs=x_spec,
        out_shape=jax.ShapeDtypeStruct(x.shape, F32),
        compiler_params=_params(("arbitrary",)),
        name="xattn_sample",
    )(x, cache_k, cache_v, norm_g, wq16, wo16)


MAX_FIR_STEPS = 2 * SUBLANES


def _causal_conv(buf_ref, w_ref, bias_row, out_ref, n_taps, tt, nb, unrolled=False):
    groups = nb // SUBLANES
    assert groups & (groups - 1) == 0
    FIR_STEPS = MAX_FIR_STEPS if tt % MAX_FIR_STEPS == 0 else SUBLANES
    assert tt % FIR_STEPS == 0
    n_in = FIR_STEPS + n_taps - 1
    n_steps = (tt // FIR_STEPS) * groups
    lane_blocks = [slice(c * LANES, (c + 1) * LANES) for c in range(D // LANES)]

    def taps(cs):
        return [jnp.broadcast_to(w_ref[k:k + 1, cs], (SUBLANES, LANES)) for k in range(n_taps)]

    def fir_step(s, cs, wk, bias):
        if isinstance(s, int):
            base = (s // groups) * (FIR_STEPS * nb) + (s % groups) * SUBLANES
        else:
            t_chunk = lax.shift_right_logical(s, groups.bit_length() - 1)
            base = t_chunk * (FIR_STEPS * nb) + (s & (groups - 1)) * SUBLANES
            base = pl.multiple_of(base, SUBLANES)
        acc = [bias] * FIR_STEPS
        for j in range(n_in):
            xj = buf_ref[pl.ds(base + j * nb, SUBLANES), cs]
            for o in range(FIR_STEPS):
                if 0 <= j - o < n_taps:
                    acc[o] = acc[o] + wk[j - o] * xj
        for o in range(FIR_STEPS):
            out_ref[pl.ds(base + o * nb, SUBLANES), cs] = acc[o]

    for cs in lane_blocks:
        wk = taps(cs)
        bias = jnp.broadcast_to(bias_row[0:1, cs], (SUBLANES, LANES))
        if unrolled:
            for s in range(n_steps):
                fir_step(s, cs, wk, bias)
        else:
            lax.fori_loop(0, n_steps, lambda s, carry: (fir_step(s, cs, wk, bias), carry)[1], 0)


def _mixer_body(*refs, tt, nb, has_state):
    if has_state:
        x_ref, ha_ref, hb_ref, h0_ref = refs[:4]
        refs = refs[4:]
    else:
        x_ref = refs[0]
        refs = refs[1:]
    (vec_ref, win_ref, caw_ref, wa_ref, cbw_ref, lruw_ref, wb_ref, wout_ref,
     o_ref, na_ref, nbuf_ref, nh_ref,
     abuf, bbuf, hbuf, conv_s) = refs

    def vec(r):
        return vec_ref[r:r + 1, :]

    rows = tt * nb
    ha_rows = (CONV_A - 1) * nb
    hb_rows = (CONV_B - 1) * nb
    ti = pl.program_id(1)

    @pl.when(ti == 0)
    def _():
        if has_state:
            abuf[0:ha_rows, :] = ha_ref[...].reshape(ha_rows, D)
            bbuf[0:hb_rows, :] = hb_ref[...].reshape(hb_rows, D)
            hbuf[...] = h0_ref[...]
        else:
            abuf[0:ha_rows, :] = jnp.zeros((ha_rows, D), F32)
            bbuf[0:hb_rows, :] = jnp.zeros((hb_rows, D), F32)
            hbuf[...] = jnp.zeros((nb, D), F32)

    if has_state:
        x = _to_time_major(x_ref[...].reshape(rows, D), nb, tt)
    else:
        x = x_ref[...]
    u = _rms(x, vec(G_MIX)).astype(BF16)

    def proj(s):
        return _dot_cols(u, win_ref, s * D, D) + vec(V_BIN + s)

    abuf[ha_rows:ha_rows + rows, :] = proj(0) * _sigmoid(proj(1))
    _causal_conv(abuf, caw_ref, vec(V_CONV_A_B), conv_s, CONV_A, tt, nb)
    abuf[0:ha_rows, :] = abuf[rows:rows + ha_rows, :]
    ac = conv_s[...]
    mu = jnp.mean(ac, axis=-1, keepdims=True)
    acc = ac - mu
    var = jnp.mean(acc * acc, axis=-1, keepdims=True)
    ln = acc * lax.rsqrt(var + EPS) * vec(V_LN_G) + vec(V_LN_B)
    y_a = _dot_cols((ln * _sigmoid(ln)).astype(BF16), wa_ref, 0, D)

    bbuf[hb_rows:hb_rows + rows, :] = proj(2)
    _causal_conv(bbuf, cbw_ref, vec(V_CONV_B_B), conv_s, CONV_B, tt, nb, unrolled=True)
    bbuf[0:hb_rows, :] = bbuf[rows:rows + hb_rows, :]
    xb = conv_s[...]
    xb16 = xb.astype(BF16)
    ri = [_dot(xb16[:, n * LRU_BLOCK:(n + 1) * LRU_BLOCK], lruw_ref[n]) for n in range(LRU_BLOCKS)]
    r_pre = jnp.concatenate([p[:, :LRU_BLOCK] for p in ri], axis=-1)
    i_pre = jnp.concatenate([p[:, LRU_BLOCK:] for p in ri], axis=-1)
    r = _sigmoid(r_pre + vec(V_LRU_B_A))
    gi = _sigmoid(i_pre + vec(V_LRU_B_X))
    nlam = -vec(V_LAMBDA)
    softplus = jnp.maximum(nlam, 0.0) + jnp.log1p(jnp.exp(-jnp.abs(nlam)))
    log_a = (-LRU_C) * r * softplus
    a = jnp.exp(log_a)
    one_m_a2 = -jnp.tanh(log_a) * (a * a + 1.0)
    drive = jnp.sqrt(one_m_a2) * (gi * xb)
    h_last = hbuf[...]
    h_rows = []
    for t in range(tt):
        h_last = a[t * nb:(t + 1) * nb] * h_last + drive[t * nb:(t + 1) * nb]
        h_rows.append(h_last)
    hbuf[...] = h_last
    hs = jnp.concatenate(h_rows, axis=0)
    bg = proj(3)
    gelu = 0.5 * bg * (1.0 + jnp.tanh(math.sqrt(2.0 / math.pi) * (bg + 0.044715 * (bg * bg * bg))))
    y_b = _dot_cols((hs * gelu).astype(BF16), wb_ref, 0, D)

    merged = _sigmoid(proj(4)) * y_a + _sigmoid(proj(5)) * y_b
    m = _dot_cols(merged.astype(BF16), wout_ref, 0, D)
    y = x + _rms(m, vec(G_MIX + 1))
    if has_state:
        o_ref[...] = _to_batch_major(y, nb, tt).reshape(nb, tt, D)
    else:
        o_ref[...] = y

    @pl.when(ti == pl.num_programs(1) - 1)
    def _():
        na_ref[...] = abuf[0:ha_rows, :].reshape(CONV_A - 1, nb, D)
        nbuf_ref[...] = bbuf[0:hb_rows, :].reshape(CONV_B - 1, nb, D)
        nh_ref[...] = h_last


def _mixer(x, state, l, w, tt, nb):
    has_state = state is not None
    rows = tt * nb
    if has_state:
        n_b = x.shape[0]
        grid = (n_b // nb, 1)
        x_spec = pl.BlockSpec((nb, tt, D), lambda j, i: (j, 0, 0))
    else:
        n_b = nb
        grid = (1, x.shape[0] // rows)
        x_spec = pl.BlockSpec((rows, D), lambda j, i: (i, 0))

    def state_spec(n):
        return pl.BlockSpec((n, nb, D), lambda j, i: (0, j, 0), pipeline_mode=pl.Buffered(1))

    h_spec = pl.BlockSpec((nb, D), lambda j, i: (j, 0))
    weights = [w["vecs"], w["w_in"], w["conv_a_w"], w["w_a_out"], w["conv_b_w"], w["lru_w"],
               w["w_b_out"], w["w_out"]]
    state_in = list(state) if has_state else []
    state_specs = [state_spec(CONV_A - 1), state_spec(CONV_B - 1), h_spec]
    state_in_specs = [pl.BlockSpec((None, CONV_A - 1, nb, D), lambda j, i: (l, 0, j, 0),
                                   pipeline_mode=pl.Buffered(1)),
                      pl.BlockSpec((None, CONV_B - 1, nb, D), lambda j, i: (l, 0, j, 0)),
                      pl.BlockSpec((None, nb, D), lambda j, i: (l, j, 0))]
    return pl.pallas_call(
        functools.partial(_mixer_body, tt=tt, nb=nb, has_state=has_state),
        grid=grid,
        in_specs=[x_spec] + (state_in_specs if has_state else [])
                 + [_layer(a.shape[1:], l) for a in weights],
        out_specs=[x_spec] + state_specs,
        out_shape=[jax.ShapeDtypeStruct(x.shape, F32),
                   jax.ShapeDtypeStruct((CONV_A - 1, n_b, D), F32),
                   jax.ShapeDtypeStruct((CONV_B - 1, n_b, D), F32),
                   jax.ShapeDtypeStruct((n_b, D), F32)],
        scratch_shapes=[pltpu.VMEM(((CONV_A - 1) * nb + rows, D), F32),
                        pltpu.VMEM(((CONV_B - 1) * nb + rows, D), F32),
                        pltpu.VMEM((nb, D), F32),
                        pltpu.VMEM((rows, D), F32)],
        compiler_params=_params(("arbitrary", "arbitrary")),
        name="mixer",
    )(x, *state_in, *weights)


def kernel(x_prompt, x_sample, mem_prompt, cache_mem_k, cache_mem_v, state_conv_a, state_conv_b, state_lru_h, w_in, b_in, conv_a_w, conv_a_b, conv_ln_g, conv_ln_b, w_a_out, conv_b_w, conv_b_b, lru_w_a, lru_b_a, lru_w_x, lru_b_x, lru_lambda, w_b_out, w_out, xa_w_q, xa_w_k, xa_w_v, xa_w_o, ffn1_w_in, ffn1_w_out, ffn2_w_in, ffn2_w_out, norm_g):
    n_b, n_t, _ = x_prompt.shape
    s_b, s_t, _ = x_sample.shape

    def vec(a):
        return a.reshape(DEPTH, 1, a.shape[-1])

    vec_rows = [norm_g, vec(conv_a_b), vec(conv_ln_g), vec(conv_ln_b), vec(conv_b_b), vec(lru_b_a),
                vec(lru_b_x), vec(lru_lambda), b_in.reshape(DEPTH, 6, D)]
    n_rows = sum(a.shape[1] for a in vec_rows)
    vec_rows.append(jnp.zeros((DEPTH, N_VEC_ROWS - n_rows, D), F32))
    mixer_w = dict(
        vecs=jnp.concatenate(vec_rows, axis=1), w_in=w_in.astype(BF16), conv_a_w=conv_a_w,
        w_a_out=w_a_out.astype(BF16), conv_b_w=conv_b_w,
        lru_w=jnp.concatenate([lru_w_a, lru_w_x], axis=-1).astype(BF16),
        w_b_out=w_b_out.astype(BF16), w_out=w_out.astype(BF16))
    ffn_w = {G_FFN1: (ffn1_w_in, ffn1_w_out), G_FFN2: (ffn2_w_in, ffn2_w_out)}
    wq16 = xa_w_q.reshape(DEPTH, D, D).astype(BF16)
    wo16 = xa_w_o.reshape(DEPTH, D, D).astype(BF16)

    mem_k, mem_v, k16, v16 = _memkv(mem_prompt.reshape(n_b * N_MEM, D), norm_g,
                                    xa_w_k.reshape(DEPTH, D, D).astype(BF16),
                                    xa_w_v.reshape(DEPTH, D, D).astype(BF16))

    tm = PROMPT_ROW_TILE
    p_a, p_b, p_h = [], [], []
    x = x_prompt
    for l in range(DEPTH):
        x = _ffn(x, l, G_FFN1, norm_g, *ffn_w[G_FFN1], tm=tm, relayout="in" if l == 0 else None)
        x, na, nbuf, nh = _mixer(x, None, l, mixer_w, tt=tm // n_b, nb=n_b)
        p_a.append(na)
        p_b.append(nbuf)
        p_h.append(nh)
        x = _xattn_prompt(x, l, norm_g, k16, v16, wq16, wo16, nb=n_b, tt=XATTN_PROMPT_STEPS)
        x = _ffn(x, l, G_FFN2, norm_g, *ffn_w[G_FFN2], tm=tm,
                 relayout="out" if l == DEPTH - 1 else None, n_batch=n_b)
    y_prompt = x

    ts = SAMPLE_ROW_TILE
    s_a, s_bb, s_h = [], [], []
    state_tm = (state_conv_a.transpose(0, 2, 1, 3), state_conv_b.transpose(0, 2, 1, 3), state_lru_h)
    x = x_sample.reshape(s_b * s_t, D)
    for l in range(DEPTH):
        x = _ffn(x, l, G_FFN1, norm_g, *ffn_w[G_FFN1], tm=ts)
        x, na, nbuf, nh = _mixer(x.reshape(s_b, s_t, D), state_tm, l, mixer_w, tt=s_t,
                                 nb=MIXER_SAMPLE_BATCH)
        s_a.append(na)
        s_bb.append(nbuf)
        s_h.append(nh)
        x = _xattn_sample(x.reshape(s_b * s_t, D), l, norm_g, cache_mem_k, cache_mem_v, wq16, wo16,
                          tq=s_t, bb=XATTN_SAMPLE_BATCH)
        x = _ffn(x, l, G_FFN2, norm_g, *ffn_w[G_FFN2], tm=ts)
    y_sample = x.reshape(s_b, s_t, D)

    def batch_major(states):
        return jnp.stack(states).transpose(0, 2, 1, 3)

    return (y_prompt, y_sample, mem_k, mem_v,
            batch_major(p_a), batch_major(p_b), jnp.stack(p_h),
            batch_major(s_a), batch_major(s_bb), jnp.stack(s_h))
```
